```python
import math
import jax
import jax.numpy as jnp
from jax import lax
import numpy as np

D_MODEL = 1024
BATCH = 2
SEQ = 16384
DEPTH = 4

A_HEADS = 4
A_DK = 128
A_DV = 256
A_QK = A_HEADS * A_DK
A_V = A_HEADS * A_DV
GATE_SOFT_CAP = 15.0
B_HEADS = 8
B_DK = 128
B_DV = 128
B_WIDTH = B_HEADS * B_DK
CONV_K = 4
CHUNK = 64
D_FF = ((8 * D_MODEL // 3 + 255) // 256) * 256
DEEPNORM_ALPHA = (2.0 * DEPTH) ** 0.25
DEEPNORM_BETA = (8.0 * DEPTH) ** -0.25
LN_EPS = 1e-5
RMS_EPS = 1e-6
IN_SIZES = (A_QK, A_QK, A_V, A_V, A_HEADS, A_HEADS, 3 * B_WIDTH, B_WIDTH, B_HEADS, B_HEADS, D_MODEL, D_MODEL)
N_IN = sum(IN_SIZES)
SPLIT_POINTS = tuple(int(v) for v in np.cumsum(IN_SIZES)[:-1])

kernel_name = 'hybrid_mlstm_gdn_deepnorm'


def _layernorm(x, g, b):
    xf = x.astype(jnp.float32)
    mu = jnp.mean(xf, -1, keepdims=True)
    var = jnp.mean(jnp.square(xf - mu), -1, keepdims=True)
    y = (xf - mu) * lax.rsqrt(var + LN_EPS) * g.astype(jnp.float32) + b.astype(jnp.float32)
    return y.astype(x.dtype)


def _head_rmsnorm(h, g):
    return h * lax.rsqrt(jnp.mean(jnp.square(h), -1, keepdims=True) + RMS_EPS) * g.astype(jnp.float32)


def _soft_cap(x):
    return GATE_SOFT_CAP * jnp.tanh(x / GATE_SOFT_CAP)


def _to_chunks(t):
    bsz, s, h = t.shape[:3]
    t = t.reshape((bsz, s // CHUNK, CHUNK, h) + t.shape[3:])
    perm = (1, 0, 3, 2) + tuple(range(4, t.ndim))
    return t.transpose(perm)


def _from_chunks(t):
    nc, bsz, h, l, d = t.shape
    return t.transpose(1, 0, 3, 2, 4).reshape(bsz, nc * l, h, d)


def _causal_conv(u, w):
    k, c = w.shape
    return lax.conv_general_dilated(u, w.astype(u.dtype)[:, None, :], window_strides=(1,), padding=[(k - 1, 0)],
                                    dimension_numbers=('NWC', 'WIO', 'NWC'), feature_group_count=c)


def _mlstm_chunked(q, k, v, i_pre, f_pre):
    bsz, _, h, dk = q.shape
    dv = v.shape[-1]
    q = q * (dk ** -0.5)
    li = _soft_cap(i_pre)
    lf = jax.nn.log_sigmoid(_soft_cap(f_pre))
    xs = (_to_chunks(q), _to_chunks(k), _to_chunks(v), _to_chunks(li), _to_chunks(lf))
    causal = jnp.tril(jnp.ones((CHUNK, CHUNK), dtype=bool))

    def step(carry, chunk):
        c_st, n_st, m_st = carry
        qc, kc, vc, lic, lfc = chunk
        g = jnp.cumsum(lfc, -1)
        dmat = jnp.where(causal, g[..., :, None] - g[..., None, :] + lic[..., None, :], -jnp.inf)
        inter = g + m_st[..., None]
        m_t = jnp.maximum(inter, jnp.max(dmat, -1))
        w = jnp.exp(dmat - m_t[..., None])
        s = jnp.einsum('bhtd,bhsd->bhts', qc, kc) * w
        sc = jnp.exp(inter - m_t)
        num = jnp.einsum('bhts,bhsv->bhtv', s, vc) + sc[..., None] * jnp.einsum('bhtd,bhdv->bhtv', qc, c_st)
        den = jnp.sum(s, -1) + sc * jnp.einsum('bhtd,bhd->bht', qc, n_st)
        h_out = num / jnp.maximum(jnp.abs(den), jnp.exp(-m_t))[..., None]
        g_end = g[..., -1]
        dec = g_end[..., None] - g + lic
        m_new = jnp.maximum(g_end + m_st, jnp.max(dec, -1))
        wk = jnp.exp(dec - m_new[..., None])
        carry_scale = jnp.exp(g_end + m_st - m_new)
        c_new = carry_scale[..., None, None] * c_st + jnp.einsum('bhs,bhsd,bhsv->bhdv', wk, kc, vc)
        n_new = carry_scale[..., None] * n_st + jnp.einsum('bhs,bhsd->bhd', wk, kc)
        return (c_new, n_new, m_new), h_out

    init = (jnp.zeros((bsz, h, dk, dv), jnp.float32), jnp.zeros((bsz, h, dk), jnp.float32),
            jnp.zeros((bsz, h), jnp.float32))
    _, hs = lax.scan(step, init, xs)
    return _from_chunks(hs)


def _l2norm(t):
    return t * lax.rsqrt(jnp.sum(jnp.square(t), -1, keepdims=True) + RMS_EPS)


def _gated_delta_chunked(q, k, v, beta, g):
    bsz, _, h, dk = q.shape
    dv = v.shape[-1]
    q = _l2norm(q) * (dk ** -0.5)
    k = _l2norm(k)
    qc, kc, vc = _to_chunks(q), _to_chunks(k), _to_chunks(v)
    bc, gc = _to_chunks(beta), _to_chunks(g)
    gcum = jnp.cumsum(gc, -1)
    incl = jnp.tril(jnp.ones((CHUNK, CHUNK), dtype=bool))
    strict = jnp.tril(jnp.ones((CHUNK, CHUNK), dtype=bool), -1)
    diff = gcum[..., :, None] - gcum[..., None, :]
    decay_mat = jnp.where(incl, jnp.exp(jnp.where(incl, diff, 0.0)), 0.0)
    kk = jnp.einsum('nbhid,nbhjd->nbhij', kc, kc)
    a_low = jnp.where(strict, bc[..., :, None] * kk * decay_mat, 0.0)
    rhs = jnp.concatenate([vc * bc[..., None], kc * (bc * jnp.exp(gcum))[..., None]], -1)
    sol = lax.linalg.triangular_solve(jnp.eye(CHUNK, dtype=jnp.float32) + a_low, rhs, left_side=True, lower=True,
                                      transpose_a=False, conjugate_a=False, unit_diagonal=True)
    u, w = sol[..., :dv], sol[..., dv:]
    attn = jnp.einsum('nbhid,nbhjd->nbhij', qc, kc) * decay_mat
    qe = qc * jnp.exp(gcum)[..., None]
    kd = kc * jnp.exp(gcum[..., -1:] - gcum)[..., None]
    g_end = jnp.exp(gcum[..., -1])

    def step(s_st, chunk):
        uc, wc, ac, qec, kdc, gec = chunk
        v_new = uc - jnp.einsum('bhld,bhdv->bhlv', wc, s_st)
        o = jnp.einsum('bhld,bhdv->bhlv', qec, s_st) + jnp.einsum('bhij,bhjv->bhiv', ac, v_new)
        s_new = s_st * gec[..., None, None] + jnp.einsum('bhld,bhlv->bhdv', kdc, v_new)
        return s_new, o

    init = jnp.zeros((bsz, h, dk, dv), jnp.float32)
    _, os_ = lax.scan(step, init, (u, w, attn, qe, kd, g_end))
    return _from_chunks(os_)


def _layer(x, w_in, b_igate, b_fgate, g_mlstm_norm, conv_w, a_log, dt_bias, g_gdn_norm,
           w_branch_a, w_branch_b, w_out, ln1_g, ln1_b, w_ffn_up, w_ffn_down, ln2_g, ln2_b):
    bsz, s, _ = x.shape
    f32 = jnp.float32
    proj = jnp.einsum('bsd,dn->bsn', x, w_in)
    qa, ka, va, oa, ia, fa, qkv_b, zb, beta_b, a_b, gate_a, gate_b = jnp.split(proj, SPLIT_POINTS, axis=-1)

    ha = _mlstm_chunked(qa.reshape(bsz, s, A_HEADS, A_DK).astype(f32),
                        ka.reshape(bsz, s, A_HEADS, A_DK).astype(f32),
                        va.reshape(bsz, s, A_HEADS, A_DV).astype(f32),
                        ia.astype(f32) + b_igate.astype(f32),
                        fa.astype(f32) + b_fgate.astype(f32))
    ha = _head_rmsnorm(ha, g_mlstm_norm.reshape(A_HEADS, A_DV)).reshape(bsz, s, A_V)
    ha = (ha * jax.nn.sigmoid(oa.astype(f32))).astype(x.dtype)

    qkv = jax.nn.silu(_causal_conv(qkv_b, conv_w))
    qb, kb, vb = jnp.split(qkv, [B_WIDTH, 2 * B_WIDTH], axis=-1)
    beta = jax.nn.sigmoid(beta_b.astype(f32))
    g = -jnp.exp(a_log.astype(f32)) * jax.nn.softplus(a_b.astype(f32) + dt_bias.astype(f32))
    hb = _gated_delta_chunked(qb.reshape(bsz, s, B_HEADS, B_DK).astype(f32),
                              kb.reshape(bsz, s, B_HEADS, B_DK).astype(f32),
                              vb.reshape(bsz, s, B_HEADS, B_DV).astype(f32), beta, g)
    hb = _head_rmsnorm(hb, g_gdn_norm).reshape(bsz, s, B_WIDTH)
    hb = (hb * jax.nn.silu(zb.astype(f32))).astype(x.dtype)

    y = (jax.nn.sigmoid(gate_a) * jnp.einsum('bsc,cd->bsd', ha, w_branch_a)
         + jax.nn.sigmoid(gate_b) * jnp.einsum('bsc,cd->bsd', hb, w_branch_b))
    mix = jnp.einsum('bsd,de->bse', y, w_out)
    x = _layernorm(DEEPNORM_ALPHA * x + mix, ln1_g, ln1_b)

    gu = jnp.einsum('bsd,df->bsf', x, w_ffn_up)
    gt, up = jnp.split(gu, [D_FF], axis=-1)
    ffn = jnp.einsum('bsf,fd->bsd', jax.nn.silu(gt) * up, w_ffn_down)
    x = _layernorm(DEEPNORM_ALPHA * x + ffn, ln2_g, ln2_b)
    return x


def setup_inputs(seed: int = 0) -> dict:
    key = jax.random.key(seed)
    ks = jax.random.split(key, 20)
    f32 = jnp.float32

    def nrm(k, shape, scale):
        return jax.random.normal(k, shape, f32) * scale

    x = nrm(ks[0], (BATCH, SEQ, D_MODEL), 1.0)
    w_in = nrm(ks[1], (DEPTH, D_MODEL, N_IN), D_MODEL ** -0.5)
    b_igate = -2.0 + nrm(ks[2], (DEPTH, A_HEADS), 0.3)
    b_fgate = 3.0 + jax.random.uniform(ks[3], (DEPTH, A_HEADS), f32, 0.0, 3.0)
    g_mlstm_norm = 1.0 + nrm(ks[4], (DEPTH, A_V), 0.02)
    conv_w = nrm(ks[5], (DEPTH, CONV_K, 3 * B_WIDTH), CONV_K ** -0.5)
    a_log = jnp.log(jax.random.uniform(ks[6], (DEPTH, B_HEADS), f32, 1.0, 16.0))
    dt = jnp.exp(jax.random.uniform(ks[7], (DEPTH, B_HEADS), f32, math.log(1e-3), math.log(1e-1)))
    dt_bias = dt + jnp.log(-jnp.expm1(-dt))
    g_gdn_norm = 1.0 + nrm(ks[8], (DEPTH, B_DV), 0.02)
    w_branch_a = nrm(ks[9], (DEPTH, A_V, D_MODEL), A_V ** -0.5)
    w_branch_b = nrm(ks[10], (DEPTH, B_WIDTH, D_MODEL), B_WIDTH ** -0.5)
    w_out = nrm(ks[11], (DEPTH, D_MODEL, D_MODEL), D_MODEL ** -0.5 * DEEPNORM_BETA)
    ln1_g = 1.0 + nrm(ks[12], (DEPTH, D_MODEL), 0.02)
    ln1_b = nrm(ks[13], (DEPTH, D_MODEL), 0.02)
    w_ffn_up = nrm(ks[14], (DEPTH, D_MODEL, 2 * D_FF), D_MODEL ** -0.5)
    w_ffn_down = nrm(ks[15], (DEPTH, D_FF, D_MODEL), D_FF ** -0.5 * DEEPNORM_BETA)
    ln2_g = 1.0 + nrm(ks[16], (DEPTH, D_MODEL), 0.02)
    ln2_b = nrm(ks[17], (DEPTH, D_MODEL), 0.02)
    return {'x': x, 'w_in': w_in, 'b_igate': b_igate, 'b_fgate': b_fgate, 'g_mlstm_norm': g_mlstm_norm,
            'conv_w': conv_w, 'a_log': a_log, 'dt_bias': dt_bias, 'g_gdn_norm': g_gdn_norm,
            'w_branch_a': w_branch_a, 'w_branch_b': w_branch_b, 'w_out': w_out, 'ln1_g': ln1_g, 'ln1_b': ln1_b,
            'w_ffn_up': w_ffn_up, 'w_ffn_down': w_ffn_down, 'ln2_g': ln2_g, 'ln2_b': ln2_b}


def reference(x, w_in, b_igate, b_fgate, g_mlstm_norm, conv_w, a_log, dt_bias, g_gdn_norm,
              w_branch_a, w_branch_b, w_out, ln1_g, ln1_b, w_ffn_up, w_ffn_down, ln2_g, ln2_b):
    for l in range(DEPTH):
        x = _layer(x, w_in[l], b_igate[l], b_fgate[l], g_mlstm_norm[l], conv_w[l], a_log[l], dt_bias[l],
                   g_gdn_norm[l], w_branch_a[l], w_branch_b[l], w_out[l], ln1_g[l], ln1_b[l],
                   w_ffn_up[l], w_ffn_down[l], ln2_g[l], ln2_b[l])
    return x
```

```python
import functools

import jax
import jax.numpy as jnp
from jax import lax
from jax.experimental import pallas as pl
from jax.experimental.pallas import tpu as pltpu

F32 = jnp.float32
BF16 = jnp.bfloat16

D_MODEL = 1024
DEPTH = 4
A_HEADS = 4
A_DK = 128
A_DV = 256
A_QK = A_HEADS * A_DK
A_V = A_HEADS * A_DV
GATE_SOFT_CAP = 15.0
B_HEADS = 8
B_DK = 128
B_DV = 128
B_WIDTH = B_HEADS * B_DK
CONV_K = 4
CHUNK = 64
D_FF = 2816
DEEPNORM_ALPHA = (2.0 * DEPTH) ** 0.25
LN_EPS = 1e-5
RMS_EPS = 1e-6

P_WIDTH = 9216
GATE_LANES = 128
GATE_ROWS = 32
LANE_I, LANE_F, LANE_BETA, LANE_A = 0, 4, 8, 16
HEAD_GROUP = 4
STACK = HEAD_GROUP * CHUNK

VMEM_LIMIT = 56 * 1024 * 1024


def _dot(a, b):
    return jnp.dot(a, b, preferred_element_type=F32)


def _dot_nt(a, b):
    return lax.dot_general(a, b, (((1,), (1,)), ((), ())), preferred_element_type=F32)


def _dot_tn(a, b):
    return lax.dot_general(a, b, (((0,), (0,)), ((), ())), preferred_element_type=F32)


def _split2(x):
    hi = x.astype(BF16)
    lo = (x - hi.astype(F32)).astype(BF16)
    return hi, lo


def _split3(x):
    x1 = x.astype(BF16)
    r1 = x - x1.astype(F32)
    x2 = r1.astype(BF16)
    x3 = (r1 - x2.astype(F32)).astype(BF16)
    return x1, x2, x3


def _dot3(a, b):
    ah, al = _split2(a)
    bh, bl = _split2(b)
    return _dot(ah, bh) + (_dot(ah, bl) + _dot(al, bh))


def _dot_sel_right(x, m):
    x1, x2, x3 = _split3(x)
    return _dot(x1, m) + (_dot(x2, m) + _dot(x3, m))


def _dot_sel_left(m, x):
    x1, x2, x3 = _split3(x)
    return _dot(m, x1) + (_dot(m, x2) + _dot(m, x3))


def _soft_cap(x):
    return GATE_SOFT_CAP * jnp.tanh(x / GATE_SOFT_CAP)


def _log_sigmoid(x):
    return jnp.minimum(x, 0.0) - jnp.log1p(jnp.exp(-jnp.abs(x)))


def _softplus(x):
    return jnp.maximum(x, 0.0) + jnp.log1p(jnp.exp(-jnp.abs(x)))


def _chunk_constants():
    r = lax.broadcasted_iota(jnp.int32, (CHUNK, CHUNK), 0)
    c = lax.broadcasted_iota(jnp.int32, (CHUNK, CHUNK), 1)
    lower = (c <= r).astype(BF16)
    r2 = lax.broadcasted_iota(jnp.int32, (CHUNK, 2 * CHUNK), 0)
    c2 = lax.broadcasted_iota(jnp.int32, (CHUNK, 2 * CHUNK), 1) & (CHUNK - 1)
    upper2 = (r2 <= c2).astype(BF16)
    eye2 = (r2 == c2).astype(BF16)
    return lower, upper2, eye2


def _stack_masks():
    r = lax.broadcasted_iota(jnp.int32, (STACK, STACK), 0)
    c = lax.broadcasted_iota(jnp.int32, (STACK, STACK), 1)
    same = (r >> 6) == (c >> 6)
    return same & (c <= r), same & (c < r)


def _pair_rows(rows):
    lane = lax.broadcasted_iota(jnp.int32, (1, 2 * CHUNK), 1)
    first = lane < CHUNK
    return jnp.concatenate([jnp.where(first, rows[0], rows[1]), jnp.where(first, rows[2], rows[3])], axis=1)


PROJ_TM = 512
PROJ_TN = 3072
PROJ_SUB = 512


def _proj_kernel(x_ref, w_ref, o_ref):
    xb = x_ref[...].astype(BF16)
    for j in range(PROJ_TN // PROJ_SUB):
        cols = slice(j * PROJ_SUB, (j + 1) * PROJ_SUB)
        o_ref[:, cols] = _dot(xb, w_ref[:, cols])


def _proj(x, w):
    t = x.shape[0]
    return pl.pallas_call(
        _proj_kernel,
        grid=(P_WIDTH // PROJ_TN, t // PROJ_TM),
        in_specs=[pl.BlockSpec((PROJ_TM, D_MODEL), lambda j, i: (i, 0)),
                  pl.BlockSpec((D_MODEL, PROJ_TN), lambda j, i: (0, j))],
        out_specs=pl.BlockSpec((PROJ_TM, PROJ_TN), lambda j, i: (i, j)),
        out_shape=jax.ShapeDtypeStruct((t, P_WIDTH), F32),
        compiler_params=pltpu.CompilerParams(dimension_semantics=("arbitrary", "arbitrary"),
                                             vmem_limit_bytes=VMEM_LIMIT),
        name="in_proj",
    )(x, w)


def _gate_proj_kernel(x_ref, ws_ref, wst_ref, ps_ref, pst_ref):
    xb = x_ref[...].astype(BF16)
    ps_ref[...] = _dot(xb, ws_ref[...])
    for c in range(PROJ_TM // CHUNK):
        pst_ref[c] = _dot_nt(wst_ref[...], xb[c * CHUNK:(c + 1) * CHUNK])


def _gate_proj(x, ws, wst):
    t = x.shape[0]
    return pl.pallas_call(
        _gate_proj_kernel,
        grid=(t // PROJ_TM,),
        in_specs=[pl.BlockSpec((PROJ_TM, D_MODEL), lambda i: (i, 0)),
                  pl.BlockSpec((D_MODEL, GATE_LANES), lambda i: (0, 0)),
                  pl.BlockSpec((GATE_ROWS, D_MODEL), lambda i: (0, 0))],
        out_specs=[pl.BlockSpec((PROJ_TM, GATE_LANES), lambda i: (i, 0)),
                   pl.BlockSpec((PROJ_TM // CHUNK, GATE_ROWS, CHUNK), lambda i: (i, 0, 0))],
        out_shape=[jax.ShapeDtypeStruct((t, GATE_LANES), F32),
                   jax.ShapeDtypeStruct((t // CHUNK, GATE_ROWS, CHUNK), F32)],
        compiler_params=pltpu.CompilerParams(dimension_semantics=("arbitrary",),
                                             vmem_limit_bytes=VMEM_LIMIT),
        name="gate_proj",
    )(x, ws, wst)


MLSTM_TB = 512


def _mlstm_kernel(q_ref, k_ref, v_ref, o_ref, ps_ref, pst_ref, brow_ref, bcol_ref, gn_ref,
                  out_ref, c_s, n_s, m_s):
    @pl.when(pl.program_id(1) == 0)
    def _():
        c_s[...] = jnp.zeros_like(c_s)
        n_s[...] = jnp.zeros_like(n_s)
        m_s[...] = jnp.zeros_like(m_s)

    lower, upper2, eye2 = _chunk_constants()
    mask_incl, _ = _stack_masks()
    brow = brow_ref[...]
    bcol = bcol_ref[...]
    scale = A_DK ** -0.5

    def chunk(c, carry):
        rows = pl.ds(pl.multiple_of(c * CHUNK, CHUNK), CHUNK)
        cap_c = _soft_cap(ps_ref[rows, :] + brow)
        gc = _dot_sel_left(lower, _log_sigmoid(cap_c))
        cap_r = _soft_cap(pst_ref[c] + bcol)
        gr2 = _dot_sel_right(_log_sigmoid(cap_r), upper2)
        li2 = _dot_sel_right(cap_r, eye2)
        row_all = _pair_rows([gr2[LANE_F + h:LANE_F + h + 1] - li2[LANE_I + h:LANE_I + h + 1]
                              for h in range(A_HEADS)])

        gcol, licol, mst, qh, kh, vh = [], [], [], [], [], []
        for h in range(A_HEADS):
            gcol.append(gc[:, LANE_F + h:LANE_F + h + 1])
            licol.append(cap_c[:, LANE_I + h:LANE_I + h + 1])
            mst.append(m_s[h])
            qh.append(q_ref[rows, h * A_DK:(h + 1) * A_DK] * scale)
            kh.append(k_ref[rows, h * A_DK:(h + 1) * A_DK])
            vh.append(v_ref[rows, h * A_DV:(h + 1) * A_DV])

        dmat = jnp.concatenate([gcol[h] - row_all for h in range(A_HEADS)], axis=0)
        dmat = jnp.where(mask_incl, dmat, -jnp.inf)
        inter = jnp.concatenate([gcol[h] + mst[h] for h in range(A_HEADS)], axis=0)
        m_t = jnp.maximum(inter, jnp.max(dmat, axis=-1, keepdims=True))
        w = jnp.exp(dmat - m_t)
        qs = jnp.concatenate(qh, axis=0).astype(BF16)
        ks = jnp.concatenate(kh, axis=0).astype(BF16)
        vs = jnp.concatenate(vh, axis=0).astype(BF16)
        s = _dot_nt(qs, ks) * w
        sc = jnp.exp(inter - m_t)
        q_c = jnp.concatenate([_dot(qh[h].astype(BF16), c_s[h].astype(BF16)) for h in range(A_HEADS)], axis=0)
        q_n = jnp.concatenate([jnp.sum(qh[h] * n_s[h], axis=-1, keepdims=True) for h in range(A_HEADS)], axis=0)
        num = _dot(s.astype(BF16), vs) + sc * q_c
        den = jnp.sum(s, axis=-1, keepdims=True) + sc * q_n
        hout = num / jnp.maximum(jnp.abs(den), jnp.exp(-m_t))
        hn = hout * lax.rsqrt(jnp.mean(hout * hout, axis=-1, keepdims=True) + RMS_EPS)
        for h in range(A_HEADS):
            cols = slice(h * A_DV, (h + 1) * A_DV)
            out_ref[rows, cols] = (hn[h * CHUNK:(h + 1) * CHUNK] * gn_ref[:, cols]
                                   * jax.nn.sigmoid(o_ref[rows, cols]))

        for h in range(A_HEADS):
            g_end = gcol[h][CHUNK - 1:CHUNK, :]
            dec = g_end - gcol[h] + licol[h]
            m_new = jnp.maximum(g_end + mst[h], jnp.max(dec, axis=0, keepdims=True))
            wk = jnp.exp(dec - m_new)
            carry_scale = jnp.exp(g_end + mst[h] - m_new)
            kw = wk * kh[h]
            c_s[h] = carry_scale * c_s[h] + _dot_tn(kw.astype(BF16), vh[h].astype(BF16))
            n_s[h] = carry_scale * n_s[h] + jnp.sum(kw, axis=0, keepdims=True)
            m_s[h] = m_new
        return carry

    lax.fori_loop(0, MLSTM_TB // CHUNK, chunk, 0)


def _mlstm(p, ps, pst, brow, bcol, gn, batch, seq):
    t = batch * seq
    nb = seq // MLSTM_TB
    tok = lambda b, i: b * nb + i
    return pl.pallas_call(
        _mlstm_kernel,
        grid=(batch, nb),
        in_specs=[pl.BlockSpec((MLSTM_TB, A_QK), lambda b, i: (tok(b, i), 0)),
                  pl.BlockSpec((MLSTM_TB, A_QK), lambda b, i: (tok(b, i), 1)),
                  pl.BlockSpec((MLSTM_TB, A_V), lambda b, i: (tok(b, i), 1)),
                  pl.BlockSpec((MLSTM_TB, A_V), lambda b, i: (tok(b, i), 2)),
                  pl.BlockSpec((MLSTM_TB, GATE_LANES), lambda b, i: (tok(b, i), 0)),
                  pl.BlockSpec((MLSTM_TB // CHUNK, GATE_ROWS, CHUNK), lambda b, i: (tok(b, i), 0, 0)),
                  pl.BlockSpec((1, GATE_LANES), lambda b, i: (0, 0)),
                  pl.BlockSpec((GATE_ROWS, 1), lambda b, i: (0, 0)),
                  pl.BlockSpec((1, A_V), lambda b, i: (0, 0))],
        out_specs=pl.BlockSpec((MLSTM_TB, A_V), lambda b, i: (tok(b, i), 0)),
        out_shape=jax.ShapeDtypeStruct((t, A_V), F32),
        scratch_shapes=[pltpu.VMEM((A_HEADS, A_DK, A_DV), F32),
                        pltpu.VMEM((A_HEADS, 1, A_DK), F32),
                        pltpu.VMEM((A_HEADS, 1, 1), F32)],
        compiler_params=pltpu.CompilerParams(dimension_semantics=("arbitrary", "arbitrary"),
                                             vmem_limit_bytes=VMEM_LIMIT),
        name="mlstm",
    )(p, p, p, p, ps, pst, brow, bcol, gn)


GDN_TB = 256
CONV_PAD = 8


def _gdn_kernel(qkv_ref, z_ref, ps_ref, pst_ref, cw_ref, brow_ref, bcol_ref, alrow_ref, alcol_ref, gn_ref,
                out_ref, xpad_s, act_s, s_s):
    @pl.when(pl.program_id(1) == 0)
    def _():
        xpad_s[0:CONV_PAD, :] = jnp.zeros((CONV_PAD, 3 * B_WIDTH), F32)
        s_s[...] = jnp.zeros_like(s_s)

    xpad_s[CONV_PAD:CONV_PAD + GDN_TB, :] = qkv_ref[...]
    for j in range(3 * B_HEADS):
        cols = slice(j * B_DK, (j + 1) * B_DK)
        acc = None
        for kk in range(CONV_K):
            term = xpad_s[CONV_PAD - (CONV_K - 1) + kk:CONV_PAD - (CONV_K - 1) + kk + GDN_TB, cols] * cw_ref[kk:kk + 1, cols]
            acc = term if acc is None else acc + term
        a = acc * jax.nn.sigmoid(acc)
        if j < 2 * B_HEADS:
            a = a * lax.rsqrt(jnp.sum(a * a, axis=-1, keepdims=True) + RMS_EPS)
        if j < B_HEADS:
            a = a * (B_DK ** -0.5)
        act_s[:, cols] = a
    xpad_s[0:CONV_PAD, :] = xpad_s[GDN_TB:GDN_TB + CONV_PAD, :]

    lower, upper2, _ = _chunk_constants()
    mask_incl, mask_strict = _stack_masks()
    rr = lax.broadcasted_iota(jnp.int32, (STACK, STACK), 0)
    cc = lax.broadcasted_iota(jnp.int32, (STACK, STACK), 1)
    eye = (rr == cc).astype(F32)
    brow = brow_ref[...]
    bcol = bcol_ref[...]
    neg_a_row = -jnp.exp(alrow_ref[...])
    neg_a_col = -jnp.exp(alcol_ref[...])
    gn = gn_ref[...]

    def chunk(c, carry):
        rows = pl.ds(pl.multiple_of(c * CHUNK, CHUNK), CHUNK)
        pre_c = ps_ref[rows, :]
        beta_c = jax.nn.sigmoid(pre_c)
        gc = _dot_sel_left(lower, neg_a_row * _softplus(pre_c + brow))
        gl_r = neg_a_col * _softplus(pst_ref[c] + bcol)
        gr2 = _dot_sel_right(gl_r, upper2)

        for grp in range(B_HEADS // HEAD_GROUP):
            heads = range(grp * HEAD_GROUP, (grp + 1) * HEAD_GROUP)
            row_all = _pair_rows([gr2[LANE_A + h:LANE_A + h + 1] for h in heads])
            gcol, beta, qh, kh, vh = [], [], [], [], []
            for h in heads:
                gcol.append(gc[:, LANE_A + h:LANE_A + h + 1])
                beta.append(beta_c[:, LANE_BETA + h:LANE_BETA + h + 1])
                qh.append(act_s[rows, h * B_DK:(h + 1) * B_DK])
                kh.append(act_s[rows, B_WIDTH + h * B_DK:B_WIDTH + (h + 1) * B_DK])
                vh.append(act_s[rows, 2 * B_WIDTH + h * B_DV:2 * B_WIDTH + (h + 1) * B_DV])
            diff = jnp.concatenate([g - row_all for g in gcol], axis=0)
            decay = jnp.where(mask_incl, jnp.exp(jnp.where(mask_incl, diff, 0.0)), 0.0)
            qs = jnp.concatenate(qh, axis=0).astype(BF16)
            ks = jnp.concatenate(kh, axis=0).astype(BF16)
            kk = _dot_nt(ks, ks)
            a_low = jnp.concatenate([beta[n] * kk[n * CHUNK:(n + 1) * CHUNK] for n in range(HEAD_GROUP)], axis=0)
            a_low = jnp.where(mask_strict, a_low * decay, 0.0)
            pw = -a_low
            tinv = eye + pw
            for _ in range(5):
                pw = _dot3(pw, pw)
                tinv = tinv + _dot3(tinv, pw)
            rhs = jnp.concatenate(
                [jnp.concatenate([vh[n] * beta[n], kh[n] * (beta[n] * jnp.exp(gcol[n]))], axis=1)
                 for n in range(HEAD_GROUP)], axis=0)
            sol = _dot3(tinv, rhs)
            attn = _dot_nt(qs, ks) * decay

            v_new, q_s = [], []
            for n, h in enumerate(heads):
                s_b = s_s[h].astype(BF16)
                u = sol[n * CHUNK:(n + 1) * CHUNK, 0:B_DV]
                wmat = sol[n * CHUNK:(n + 1) * CHUNK, B_DV:2 * B_DV]
                v_new.append(u - _dot(wmat.astype(BF16), s_b))
                q_s.append(_dot((qh[n] * jnp.exp(gcol[n])).astype(BF16), s_b))
            v_new_b = jnp.concatenate(v_new, axis=0).astype(BF16)
            o = jnp.concatenate(q_s, axis=0) + _dot(attn.astype(BF16), v_new_b)
            on = o * lax.rsqrt(jnp.mean(o * o, axis=-1, keepdims=True) + RMS_EPS) * gn
            for n, h in enumerate(heads):
                cols = slice(h * B_DV, (h + 1) * B_DV)
                zz = z_ref[rows, cols]
                out_ref[rows, cols] = on[n * CHUNK:(n + 1) * CHUNK] * (zz * jax.nn.sigmoid(zz))
                g_end = gcol[n][CHUNK - 1:CHUNK, :]
                kd = kh[n] * jnp.exp(g_end - gcol[n])
                s_s[h] = s_s[h] * jnp.exp(g_end) + _dot_tn(kd.astype(BF16), v_new_b[n * CHUNK:(n + 1) * CHUNK])
        return carry

    lax.fori_loop(0, GDN_TB // CHUNK, chunk, 0)


def _gdn(p, ps, pst, conv_w, brow, bcol, alrow, alcol, gn, batch, seq):
    t = batch * seq
    nb = seq // GDN_TB
    tok = lambda b, i: b * nb + i
    const2 = lambda b, i: (0, 0)
    return pl.pallas_call(
        _gdn_kernel,
        grid=(batch, nb),
        in_specs=[pl.BlockSpec((GDN_TB, 3 * B_WIDTH), lambda b, i: (tok(b, i), 1)),
                  pl.BlockSpec((GDN_TB, B_WIDTH), lambda b, i: (tok(b, i), 6)),
                  pl.BlockSpec((GDN_TB, GATE_LANES), lambda b, i: (tok(b, i), 0)),
                  pl.BlockSpec((GDN_TB // CHUNK, GATE_ROWS, CHUNK), lambda b, i: (tok(b, i), 0, 0)),
                  pl.BlockSpec((CONV_K, 3 * B_WIDTH), const2),
                  pl.BlockSpec((1, GATE_LANES), const2),
                  pl.BlockSpec((GATE_ROWS, 1), const2),
                  pl.BlockSpec((1, GATE_LANES), const2),
                  pl.BlockSpec((GATE_ROWS, 1), const2),
                  pl.BlockSpec((1, B_DV), const2)],
        out_specs=pl.BlockSpec((GDN_TB, B_WIDTH), lambda b, i: (tok(b, i), 0)),
        out_shape=jax.ShapeDtypeStruct((t, B_WIDTH), F32),
        scratch_shapes=[pltpu.VMEM((CONV_PAD + GDN_TB, 3 * B_WIDTH), F32),
                        pltpu.VMEM((GDN_TB, 3 * B_WIDTH), F32),
                        pltpu.VMEM((B_HEADS, B_DK, B_DV), F32)],
        compiler_params=pltpu.CompilerParams(dimension_semantics=("arbitrary", "arbitrary"),
                                             vmem_limit_bytes=VMEM_LIMIT),
        name="gdn",
    )(p, p, ps, pst, conv_w, brow, bcol, alrow, alcol, gn)


MERGE_TM = 256
FFN_TM = 256
FFN_SUB = 256


def _layernorm(r, g, b):
    mu = jnp.mean(r, axis=-1, keepdims=True)
    d = r - mu
    var = jnp.mean(d * d, axis=-1, keepdims=True)
    return d * lax.rsqrt(var + LN_EPS) * g + b


def _merge_kernel(ha_ref, hb_ref, ga_ref, gb_ref, x_ref, wa_ref, wb_ref, wo_ref, g_ref, b_ref, out_ref):
    ya = _dot(ha_ref[...].astype(BF16), wa_ref[...])
    yb = _dot(hb_ref[...].astype(BF16), wb_ref[...])
    y = jax.nn.sigmoid(ga_ref[...]) * ya + jax.nn.sigmoid(gb_ref[...]) * yb
    mix = _dot(y.astype(BF16), wo_ref[...])
    out_ref[...] = _layernorm(DEEPNORM_ALPHA * x_ref[...] + mix, g_ref[...], b_ref[...])


def _merge(ha, hb, p, x, wa, wb, wo, g, b):
    t = x.shape[0]
    row = lambda i: (i, 0)
    const = lambda i: (0, 0)
    wspec = pl.BlockSpec((D_MODEL, D_MODEL), const)
    vspec = pl.BlockSpec((1, D_MODEL), const)
    return pl.pallas_call(
        _merge_kernel,
        grid=(t // MERGE_TM,),
        in_specs=[pl.BlockSpec((MERGE_TM, A_V), row),
                  pl.BlockSpec((MERGE_TM, B_WIDTH), row),
                  pl.BlockSpec((MERGE_TM, D_MODEL), lambda i: (i, 7)),
                  pl.BlockSpec((MERGE_TM, D_MODEL), lambda i: (i, 8)),
                  pl.BlockSpec((MERGE_TM, D_MODEL), row),
                  wspec, wspec, wspec, vspec, vspec],
        out_specs=pl.BlockSpec((MERGE_TM, D_MODEL), row),
        out_shape=jax.ShapeDtypeStruct((t, D_MODEL), F32),
        compiler_params=pltpu.CompilerParams(dimension_semantics=("arbitrary",),
                                             vmem_limit_bytes=VMEM_LIMIT),
        name="merge_ln",
    )(ha, hb, p, p, x, wa, wb, wo, g, b)


def _ffn_kernel(x_ref, wu_ref, wd_ref, g_ref, b_ref, out_ref):
    x = x_ref[...]
    xb = x.astype(BF16)
    acc = None
    for j in range(D_FF // FFN_SUB):
        gt = _dot(xb, wu_ref[:, j * FFN_SUB:(j + 1) * FFN_SUB])
        up = _dot(xb, wu_ref[:, D_FF + j * FFN_SUB:D_FF + (j + 1) * FFN_SUB])
        hid = (gt * jax.nn.sigmoid(gt) * up).astype(BF16)
        part = _dot(hid, wd_ref[j * FFN_SUB:(j + 1) * FFN_SUB, :])
        acc = part if acc is None else acc + part
    out_ref[...] = _layernorm(DEEPNORM_ALPHA * x + acc, g_ref[...], b_ref[...])


def _ffn(x, wu, wd, g, b):
    t = x.shape[0]
    row = lambda i: (i, 0)
    const = lambda i: (0, 0)
    return pl.pallas_call(
        _ffn_kernel,
        grid=(t // FFN_TM,),
        in_specs=[pl.BlockSpec((FFN_TM, D_MODEL), row),
                  pl.BlockSpec((D_MODEL, 2 * D_FF), const, pipeline_mode=pl.Buffered(1)),
                  pl.BlockSpec((D_FF, D_MODEL), const, pipeline_mode=pl.Buffered(1)),
                  pl.BlockSpec((1, D_MODEL), const),
                  pl.BlockSpec((1, D_MODEL), const)],
        out_specs=pl.BlockSpec((FFN_TM, D_MODEL), row),
        out_shape=jax.ShapeDtypeStruct((t, D_MODEL), F32),
        compiler_params=pltpu.CompilerParams(dimension_semantics=("arbitrary",),
                                             vmem_limit_bytes=VMEM_LIMIT),
        name="ffn_ln",
    )(x, wu, wd, g, b)


def _gate_vec(parts, width):
    out = jnp.zeros((DEPTH, width), F32)
    for off, val in parts:
        out = out.at[:, off:off + val.shape[1]].set(val.astype(F32))
    return out


def kernel(x, w_in, b_igate, b_fgate, g_mlstm_norm, conv_w, a_log, dt_bias, g_gdn_norm, w_branch_a, w_branch_b, w_out, ln1_g, ln1_b, w_ffn_up, w_ffn_down, ln2_g, ln2_b):
    batch, seq, _ = x.shape
    t = batch * seq

    o_q, o_if, o_qkvb, o_beta, o_gate = 0, 2 * A_QK + 2 * A_V, 2 * A_QK + 2 * A_V + 2 * A_HEADS, None, None
    o_beta = o_qkvb + 4 * B_WIDTH
    o_gate = o_beta + 2 * B_HEADS
    w_big = jnp.concatenate([w_in[:, :, o_q:o_if], w_in[:, :, o_qkvb:o_beta], w_in[:, :, o_gate:]], axis=-1).astype(BF16)
    w_gates = jnp.concatenate([w_in[:, :, o_if:o_qkvb], w_in[:, :, o_beta:o_gate]], axis=-1)
    n_g = w_gates.shape[-1]
    w_small = jnp.pad(w_gates, ((0, 0), (0, 0), (0, GATE_LANES - n_g))).astype(BF16)
    w_small_t = jnp.pad(jnp.swapaxes(w_gates, 1, 2), ((0, 0), (0, GATE_ROWS - n_g), (0, 0))).astype(BF16)

    bias_row = _gate_vec([(LANE_I, b_igate), (LANE_F, b_fgate), (LANE_A, dt_bias)], GATE_LANES)
    bias_col = _gate_vec([(LANE_I, b_igate), (LANE_F, b_fgate), (LANE_A, dt_bias)], GATE_ROWS)
    alog_row = _gate_vec([(LANE_A, a_log)], GATE_LANES)
    alog_col = _gate_vec([(LANE_A, a_log)], GATE_ROWS)

    wa = w_branch_a.astype(BF16)
    wb = w_branch_b.astype(BF16)
    wo = w_out.astype(BF16)
    wu = w_ffn_up.astype(BF16)
    wd = w_ffn_down.astype(BF16)

    xt = x.reshape(t, D_MODEL)
    for l in range(DEPTH):
        p = _proj(xt, w_big[l])
        ps, pst = _gate_proj(xt, w_small[l], w_small_t[l])
        brow = bias_row[l][None, :]
        bcol = bias_col[l][:, None]
        ha = _mlstm(p, ps, pst, brow, bcol, g_mlstm_norm[l][None, :].astype(F32), batch, seq)
        hb = _gdn(p, ps, pst, conv_w[l].astype(F32), brow, bcol, alog_row[l][None, :], alog_col[l][:, None],
                  g_gdn_norm[l][None, :].astype(F32), batch, seq)
        x1 = _merge(ha, hb, p, xt, wa[l], wb[l], wo[l], ln1_g[l][None, :], ln1_b[l][None, :])
        xt = _ffn(x1, wu[l], wd[l], ln2_g[l][None, :], ln2_b[l][None, :])
    return xt.reshape(batch, seq, D_MODEL)
```

```python
import functools

import jax
import jax.numpy as jnp
from jax import lax
from jax.experimental import pallas as pl
from jax.experimental.pallas import tpu as pltpu

F32 = jnp.float32
BF16 = jnp.bfloat16

D_MODEL = 1024
DEPTH = 4
A_HEADS = 4
A_DK = 128
A_DV = 256
A_QK = A_HEADS * A_DK
A_V = A_HEADS * A_DV
GATE_SOFT_CAP = 15.0
B_HEADS = 8
B_DK = 128
B_DV = 128
B_WIDTH = B_HEADS * B_DK
CONV_K = 4
CHUNK = 64
D_FF = 2816
DEEPNORM_ALPHA = (2.0 * DEPTH) ** 0.25
LN_EPS = 1e-5
RMS_EPS = 1e-6

P_WIDTH = 9216
GATE_LANES = 128
GATE_ROWS = 32
LANE_I, LANE_F, LANE_BETA, LANE_A = 0, 4, 8, 16
HEAD_GROUP = 4
STACK = HEAD_GROUP * CHUNK

VMEM_LIMIT = 56 * 1024 * 1024


def _dot(a, b):
    return jnp.dot(a, b, preferred_element_type=F32)


def _dot_nt(a, b):
    return lax.dot_general(a, b, (((1,), (1,)), ((), ())), preferred_element_type=F32)


def _dot_tn(a, b):
    return lax.dot_general(a, b, (((0,), (0,)), ((), ())), preferred_element_type=F32)


def _split2(x):
    hi = x.astype(BF16)
    lo = (x - hi.astype(F32)).astype(BF16)
    return hi, lo


def _split3(x):
    x1 = x.astype(BF16)
    r1 = x - x1.astype(F32)
    x2 = r1.astype(BF16)
    x3 = (r1 - x2.astype(F32)).astype(BF16)
    return x1, x2, x3


def _dot3(a, b):
    ah, al = _split2(a)
    bh, bl = _split2(b)
    return _dot(ah, bh) + (_dot(ah, bl) + _dot(al, bh))


def _dot_sel_right(x, m):
    x1, x2, x3 = _split3(x)
    return _dot(x1, m) + (_dot(x2, m) + _dot(x3, m))


def _dot_sel_left(m, x):
    x1, x2, x3 = _split3(x)
    return _dot(m, x1) + (_dot(m, x2) + _dot(m, x3))


def _soft_cap(x):
    return GATE_SOFT_CAP * jnp.tanh(x / GATE_SOFT_CAP)


def _log_sigmoid(x):
    return jnp.minimum(x, 0.0) - jnp.log1p(jnp.exp(-jnp.abs(x)))


def _softplus(x):
    return jnp.maximum(x, 0.0) + jnp.log1p(jnp.exp(-jnp.abs(x)))


def _chunk_constants():
    r = lax.broadcasted_iota(jnp.int32, (CHUNK, CHUNK), 0)
    c = lax.broadcasted_iota(jnp.int32, (CHUNK, CHUNK), 1)
    lower = (c <= r).astype(BF16)
    r2 = lax.broadcasted_iota(jnp.int32, (CHUNK, 2 * CHUNK), 0)
    c2 = lax.broadcasted_iota(jnp.int32, (CHUNK, 2 * CHUNK), 1) & (CHUNK - 1)
    upper2 = (r2 <= c2).astype(BF16)
    eye2 = (r2 == c2).astype(BF16)
    return lower, upper2, eye2


def _stack_masks():
    r = lax.broadcasted_iota(jnp.int32, (STACK, STACK), 0)
    c = lax.broadcasted_iota(jnp.int32, (STACK, STACK), 1)
    same = (r >> 6) == (c >> 6)
    return same & (c <= r), same & (c < r)


def _pair_rows(rows):
    lane = lax.broadcasted_iota(jnp.int32, (1, 2 * CHUNK), 1)
    first = lane < CHUNK
    return jnp.concatenate([jnp.where(first, rows[0], rows[1]), jnp.where(first, rows[2], rows[3])], axis=1)


PROJ_TM = 512
PROJ_TN = 3072
PROJ_SUB = 512


def _proj_kernel(x_ref, w_ref, o_ref):
    xb = x_ref[...].astype(BF16)
    for j in range(PROJ_TN // PROJ_SUB):
        cols = slice(j * PROJ_SUB, (j + 1) * PROJ_SUB)
        o_ref[:, cols] = _dot(xb, w_ref[:, cols])


def _proj(x, w):
    t = x.shape[0]
    return pl.pallas_call(
        _proj_kernel,
        grid=(P_WIDTH // PROJ_TN, t // PROJ_TM),
        in_specs=[pl.BlockSpec((PROJ_TM, D_MODEL), lambda j, i: (i, 0)),
                  pl.BlockSpec((D_MODEL, PROJ_TN), lambda j, i: (0, j))],
        out_specs=pl.BlockSpec((PROJ_TM, PROJ_TN), lambda j, i: (i, j)),
        out_shape=jax.ShapeDtypeStruct((t, P_WIDTH), F32),
        compiler_params=pltpu.CompilerParams(dimension_semantics=("arbitrary", "arbitrary"),
                                             vmem_limit_bytes=VMEM_LIMIT),
        name="in_proj",
    )(x, w)


def _gate_proj_kernel(x_ref, ws_ref, wst_ref, ps_ref, pst_ref):
    xb = x_ref[...].astype(BF16)
    ps_ref[...] = _dot(xb, ws_ref[...])
    for c in range(PROJ_TM // CHUNK):
        pst_ref[c] = _dot_nt(wst_ref[...], xb[c * CHUNK:(c + 1) * CHUNK])


def _gate_proj(x, ws, wst):
    t = x.shape[0]
    return pl.pallas_call(
        _gate_proj_kernel,
        grid=(t // PROJ_TM,),
        in_specs=[pl.BlockSpec((PROJ_TM, D_MODEL), lambda i: (i, 0)),
                  pl.BlockSpec((D_MODEL, GATE_LANES), lambda i: (0, 0)),
                  pl.BlockSpec((GATE_ROWS, D_MODEL), lambda i: (0, 0))],
        out_specs=[pl.BlockSpec((PROJ_TM, GATE_LANES), lambda i: (i, 0)),
                   pl.BlockSpec((PROJ_TM // CHUNK, GATE_ROWS, CHUNK), lambda i: (i, 0, 0))],
        out_shape=[jax.ShapeDtypeStruct((t, GATE_LANES), F32),
                   jax.ShapeDtypeStruct((t // CHUNK, GATE_ROWS, CHUNK), F32)],
        compiler_params=pltpu.CompilerParams(dimension_semantics=("arbitrary",),
                                             vmem_limit_bytes=VMEM_LIMIT),
        name="gate_proj",
    )(x, ws, wst)


MLSTM_TB = 512


def _mlstm_kernel(q_ref, k_ref, v_ref, o_ref, ps_ref, pst_ref, brow_ref, bcol_ref, gn_ref,
                  out_ref, c_s, n_s, m_s):
    @pl.when(pl.program_id(1) == 0)
    def _():
        c_s[...] = jnp.zeros_like(c_s)
        n_s[...] = jnp.zeros_like(n_s)
        m_s[...] = jnp.zeros_like(m_s)

    lower, upper2, eye2 = _chunk_constants()
    mask_incl, _ = _stack_masks()
    brow = brow_ref[...]
    bcol = bcol_ref[...]
    scale = A_DK ** -0.5

    def chunk(c, carry):
        rows = pl.ds(pl.multiple_of(c * CHUNK, CHUNK), CHUNK)
        cap_c = _soft_cap(ps_ref[rows, :] + brow)
        gc = _dot_sel_left(lower, _log_sigmoid(cap_c))
        cap_r = _soft_cap(pst_ref[c] + bcol)
        gr2 = _dot_sel_right(_log_sigmoid(cap_r), upper2)
        li2 = _dot_sel_right(cap_r, eye2)
        row_all = _pair_rows([gr2[LANE_F + h:LANE_F + h + 1] - li2[LANE_I + h:LANE_I + h + 1]
                              for h in range(A_HEADS)])

        gcol, licol, mst, qh, kh, vh = [], [], [], [], [], []
        for h in range(A_HEADS):
            gcol.append(gc[:, LANE_F + h:LANE_F + h + 1])
            licol.append(cap_c[:, LANE_I + h:LANE_I + h + 1])
            mst.append(m_s[h])
            qh.append(q_ref[rows, h * A_DK:(h + 1) * A_DK] * scale)
            kh.append(k_ref[rows, h * A_DK:(h + 1) * A_DK])
            vh.append(v_ref[rows, h * A_DV:(h + 1) * A_DV])

        dmat = jnp.concatenate([gcol[h] - row_all for h in range(A_HEADS)], axis=0)
        dmat = jnp.where(mask_incl, dmat, -jnp.inf)
        inter = jnp.concatenate([gcol[h] + mst[h] for h in range(A_HEADS)], axis=0)
        m_t = jnp.maximum(inter, jnp.max(dmat, axis=-1, keepdims=True))
        w = jnp.exp(dmat - m_t)
        qs = jnp.concatenate(qh, axis=0).astype(BF16)
        ks = jnp.concatenate(kh, axis=0).astype(BF16)
        vs = jnp.concatenate(vh, axis=0).astype(BF16)
        s = _dot_nt(qs, ks) * w
        sc = jnp.exp(inter - m_t)
        q_c = jnp.concatenate([_dot(qh[h].astype(BF16), c_s[h].astype(BF16)) for h in range(A_HEADS)], axis=0)
        q_n = jnp.concatenate([jnp.sum(qh[h] * n_s[h], axis=-1, keepdims=True) for h in range(A_HEADS)], axis=0)
        num = _dot(s.astype(BF16), vs) + sc * q_c
        den = jnp.sum(s, axis=-1, keepdims=True) + sc * q_n
        hout = num / jnp.maximum(jnp.abs(den), jnp.exp(-m_t))
        hn = hout * lax.rsqrt(jnp.mean(hout * hout, axis=-1, keepdims=True) + RMS_EPS)
        for h in range(A_HEADS):
            cols = slice(h * A_DV, (h + 1) * A_DV)
            out_ref[rows, cols] = (hn[h * CHUNK:(h + 1) * CHUNK] * gn_ref[:, cols]
                                   * jax.nn.sigmoid(o_ref[rows, cols]))

        for h in range(A_HEADS):
            g_end = gcol[h][CHUNK - 1:CHUNK, :]
            dec = g_end - gcol[h] + licol[h]
            m_new = jnp.maximum(g_end + mst[h], jnp.max(dec, axis=0, keepdims=True))
            wk = jnp.exp(dec - m_new)
            carry_scale = jnp.exp(g_end + mst[h] - m_new)
            kw = wk * kh[h]
            c_s[h] = carry_scale * c_s[h] + _dot_tn(kw.astype(BF16), vh[h].astype(BF16))
            n_s[h] = carry_scale * n_s[h] + jnp.sum(kw, axis=0, keepdims=True)
            m_s[h] = m_new
        return carry

    lax.fori_loop(0, MLSTM_TB // CHUNK, chunk, 0)


def _mlstm(p, ps, pst, brow, bcol, gn, batch, seq):
    t = batch * seq
    nb = seq // MLSTM_TB
    tok = lambda b, i: b * nb + i
    return pl.pallas_call(
        _mlstm_kernel,
        grid=(batch, nb),
        in_specs=[pl.BlockSpec((MLSTM_TB, A_QK), lambda b, i: (tok(b, i), 0)),
                  pl.BlockSpec((MLSTM_TB, A_QK), lambda b, i: (tok(b, i), 1)),
                  pl.BlockSpec((MLSTM_TB, A_V), lambda b, i: (tok(b, i), 1)),
                  pl.BlockSpec((MLSTM_TB, A_V), lambda b, i: (tok(b, i), 2)),
                  pl.BlockSpec((MLSTM_TB, GATE_LANES), lambda b, i: (tok(b, i), 0)),
                  pl.BlockSpec((MLSTM_TB // CHUNK, GATE_ROWS, CHUNK), lambda b, i: (tok(b, i), 0, 0)),
                  pl.BlockSpec((1, GATE_LANES), lambda b, i: (0, 0)),
                  pl.BlockSpec((GATE_ROWS, 1), lambda b, i: (0, 0)),
                  pl.BlockSpec((1, A_V), lambda b, i: (0, 0))],
        out_specs=pl.BlockSpec((MLSTM_TB, A_V), lambda b, i: (tok(b, i), 0)),
        out_shape=jax.ShapeDtypeStruct((t, A_V), F32),
        scratch_shapes=[pltpu.VMEM((A_HEADS, A_DK, A_DV), F32),
                        pltpu.VMEM((A_HEADS, 1, A_DK), F32),
                        pltpu.VMEM((A_HEADS, 1, 1), F32)],
        compiler_params=pltpu.CompilerParams(dimension_semantics=("arbitrary", "arbitrary"),
                                             vmem_limit_bytes=VMEM_LIMIT),
        name="mlstm",
    )(p, p, p, p, ps, pst, brow, bcol, gn)


GDN_TB = 256
CONV_PAD = 8


def _gdn_kernel(qkv_ref, z_ref, ps_ref, pst_ref, cw_ref, brow_ref, bcol_ref, alrow_ref, alcol_ref, gn_ref,
                out_ref, xpad_s, act_s, s_s, u_s, attn_s, wq_s, kd_s, ge_s):
    @pl.when(pl.program_id(1) == 0)
    def _():
        xpad_s[0:CONV_PAD, :] = jnp.zeros((CONV_PAD, 3 * B_WIDTH), F32)
        s_s[...] = jnp.zeros_like(s_s)

    xpad_s[CONV_PAD:CONV_PAD + GDN_TB, :] = qkv_ref[...]
    for j in range(3 * B_HEADS):
        cols = slice(j * B_DK, (j + 1) * B_DK)
        acc = None
        for kk in range(CONV_K):
            term = xpad_s[CONV_PAD - (CONV_K - 1) + kk:CONV_PAD - (CONV_K - 1) + kk + GDN_TB, cols] * cw_ref[kk:kk + 1, cols]
            acc = term if acc is None else acc + term
        a = acc * jax.nn.sigmoid(acc)
        if j < 2 * B_HEADS:
            a = a * lax.rsqrt(jnp.sum(a * a, axis=-1, keepdims=True) + RMS_EPS)
        if j < B_HEADS:
            a = a * (B_DK ** -0.5)
        act_s[:, cols] = a
    xpad_s[0:CONV_PAD, :] = xpad_s[GDN_TB:GDN_TB + CONV_PAD, :]

    lower, upper2, _ = _chunk_constants()
    mask_incl, mask_strict = _stack_masks()
    rr = lax.broadcasted_iota(jnp.int32, (STACK, STACK), 0)
    cc = lax.broadcasted_iota(jnp.int32, (STACK, STACK), 1)
    eye = (rr == cc).astype(F32)
    brow = brow_ref[...]
    bcol = bcol_ref[...]
    neg_a_row = -jnp.exp(alrow_ref[...])
    neg_a_col = -jnp.exp(alcol_ref[...])
    gn = gn_ref[...]

    n_grp = B_HEADS // HEAD_GROUP

    for c in range(GDN_TB // CHUNK):
        rows = slice(c * CHUNK, (c + 1) * CHUNK)
        pre_c = ps_ref[rows, :]
        beta_c = jax.nn.sigmoid(pre_c)
        gc = _dot_sel_left(lower, neg_a_row * _softplus(pre_c + brow))
        gl_r = neg_a_col * _softplus(pst_ref[c] + bcol)
        gr2 = _dot_sel_right(gl_r, upper2)

        for grp in range(n_grp):
            heads = range(grp * HEAD_GROUP, (grp + 1) * HEAD_GROUP)
            row_all = _pair_rows([gr2[LANE_A + h:LANE_A + h + 1] for h in heads])
            gcol, beta, qh, kh, vh = [], [], [], [], []
            for h in heads:
                gcol.append(gc[:, LANE_A + h:LANE_A + h + 1])
                beta.append(beta_c[:, LANE_BETA + h:LANE_BETA + h + 1])
                qh.append(act_s[rows, h * B_DK:(h + 1) * B_DK])
                kh.append(act_s[rows, B_WIDTH + h * B_DK:B_WIDTH + (h + 1) * B_DK])
                vh.append(act_s[rows, 2 * B_WIDTH + h * B_DV:2 * B_WIDTH + (h + 1) * B_DV])
            diff = jnp.concatenate([g - row_all for g in gcol], axis=0)
            decay = jnp.where(mask_incl, jnp.exp(jnp.where(mask_incl, diff, 0.0)), 0.0)
            ks = jnp.concatenate(kh, axis=0).astype(BF16)
            qk_kk = _dot_nt(jnp.concatenate(qh + kh, axis=0).astype(BF16), ks)
            kk = qk_kk[STACK:2 * STACK]
            a_low = jnp.concatenate([beta[n] * kk[n * CHUNK:(n + 1) * CHUNK] for n in range(HEAD_GROUP)], axis=0)
            a_low = jnp.where(mask_strict, a_low * decay, 0.0)
            pw = -a_low
            tinv = eye + pw
            for _ in range(5):
                pw_b = pw.astype(BF16)
                pw = _dot(pw_b, pw_b)
                tinv = tinv + _dot(tinv.astype(BF16), pw.astype(BF16))
            rhs = jnp.concatenate(
                [jnp.concatenate([vh[n] * beta[n], kh[n] * (beta[n] * jnp.exp(gcol[n]))], axis=1)
                 for n in range(HEAD_GROUP)], axis=0)
            sol = _dot(tinv.astype(BF16), rhs.astype(BF16))
            u_s[c, grp] = sol[:, 0:B_DV]
            attn_s[c, grp] = (qk_kk[0:STACK] * decay).astype(BF16)
            for n, h in enumerate(heads):
                g_end = gcol[n][CHUNK - 1:CHUNK, :]
                w_h = sol[n * CHUNK:(n + 1) * CHUNK, B_DV:2 * B_DV]
                wq_s[c, h] = jnp.concatenate([w_h, qh[n] * jnp.exp(gcol[n])], axis=0).astype(BF16)
                kd_s[c, h] = (kh[n] * jnp.exp(g_end - gcol[n])).astype(BF16)
                ge_s[c, h] = jnp.broadcast_to(jnp.exp(g_end), (1, B_DV))

    for c in range(GDN_TB // CHUNK):
        rows = slice(c * CHUNK, (c + 1) * CHUNK)
        for grp in range(n_grp):
            heads = range(grp * HEAD_GROUP, (grp + 1) * HEAD_GROUP)
            u = u_s[c, grp]
            v_new, q_s = [], []
            for n, h in enumerate(heads):
                r = _dot(wq_s[c, h], s_s[h].astype(BF16))
                v_new.append(u[n * CHUNK:(n + 1) * CHUNK] - r[0:CHUNK])
                q_s.append(r[CHUNK:2 * CHUNK])
            v_new_b = jnp.concatenate(v_new, axis=0).astype(BF16)
            o = jnp.concatenate(q_s, axis=0) + _dot(attn_s[c, grp], v_new_b)
            on = o * lax.rsqrt(jnp.mean(o * o, axis=-1, keepdims=True) + RMS_EPS) * gn
            for n, h in enumerate(heads):
                cols = slice(h * B_DV, (h + 1) * B_DV)
                zz = z_ref[rows, cols]
                out_ref[rows, cols] = on[n * CHUNK:(n + 1) * CHUNK] * (zz * jax.nn.sigmoid(zz))
                s_s[h] = s_s[h] * ge_s[c, h] + _dot_tn(kd_s[c, h], v_new_b[n * CHUNK:(n + 1) * CHUNK])


def _gdn(p, ps, pst, conv_w, brow, bcol, alrow, alcol, gn, batch, seq):
    t = batch * seq
    nb = seq // GDN_TB
    nch = GDN_TB // CHUNK
    tok = lambda b, i: b * nb + i
    const2 = lambda b, i: (0, 0)
    return pl.pallas_call(
        _gdn_kernel,
        grid=(batch, nb),
        in_specs=[pl.BlockSpec((GDN_TB, 3 * B_WIDTH), lambda b, i: (tok(b, i), 1)),
                  pl.BlockSpec((GDN_TB, B_WIDTH), lambda b, i: (tok(b, i), 6)),
                  pl.BlockSpec((GDN_TB, GATE_LANES), lambda b, i: (tok(b, i), 0)),
                  pl.BlockSpec((GDN_TB // CHUNK, GATE_ROWS, CHUNK), lambda b, i: (tok(b, i), 0, 0)),
                  pl.BlockSpec((CONV_K, 3 * B_WIDTH), const2),
                  pl.BlockSpec((1, GATE_LANES), const2),
                  pl.BlockSpec((GATE_ROWS, 1), const2),
                  pl.BlockSpec((1, GATE_LANES), const2),
                  pl.BlockSpec((GATE_ROWS, 1), const2),
                  pl.BlockSpec((1, B_DV), const2)],
        out_specs=pl.BlockSpec((GDN_TB, B_WIDTH), lambda b, i: (tok(b, i), 0)),
        out_shape=jax.ShapeDtypeStruct((t, B_WIDTH), F32),
        scratch_shapes=[pltpu.VMEM((CONV_PAD + GDN_TB, 3 * B_WIDTH), F32),
                        pltpu.VMEM((GDN_TB, 3 * B_WIDTH), F32),
                        pltpu.VMEM((B_HEADS, B_DK, B_DV), F32),
                        pltpu.VMEM((nch, B_HEADS // HEAD_GROUP, STACK, B_DV), F32),
                        pltpu.VMEM((nch, B_HEADS // HEAD_GROUP, STACK, STACK), BF16),
                        pltpu.VMEM((nch, B_HEADS, 2 * CHUNK, B_DK), BF16),
                        pltpu.VMEM((nch, B_HEADS, CHUNK, B_DK), BF16),
                        pltpu.VMEM((nch, B_HEADS, 1, B_DV), F32)],
        compiler_params=pltpu.CompilerParams(dimension_semantics=("arbitrary", "arbitrary"),
                                             vmem_limit_bytes=VMEM_LIMIT),
        name="gdn",
    )(p, p, ps, pst, conv_w, brow, bcol, alrow, alcol, gn)


MERGE_TM = 512
FFN_TM = 512
FFN_SUB = 256


def _layernorm(r, g, b):
    mu = jnp.mean(r, axis=-1, keepdims=True)
    d = r - mu
    var = jnp.mean(d * d, axis=-1, keepdims=True)
    return d * lax.rsqrt(var + LN_EPS) * g + b


def _merge_kernel(ha_ref, hb_ref, ga_ref, gb_ref, x_ref, wa_ref, wb_ref, wo_ref, g_ref, b_ref, out_ref):
    ya = _dot(ha_ref[...].astype(BF16), wa_ref[...])
    yb = _dot(hb_ref[...].astype(BF16), wb_ref[...])
    y = jax.nn.sigmoid(ga_ref[...]) * ya + jax.nn.sigmoid(gb_ref[...]) * yb
    mix = _dot(y.astype(BF16), wo_ref[...])
    out_ref[...] = _layernorm(DEEPNORM_ALPHA * x_ref[...] + mix, g_ref[...], b_ref[...])


def _merge(ha, hb, p, x, wa, wb, wo, g, b):
    t = x.shape[0]
    row = lambda i: (i, 0)
    const = lambda i: (0, 0)
    wspec = pl.BlockSpec((D_MODEL, D_MODEL), const, pipeline_mode=pl.Buffered(1))
    vspec = pl.BlockSpec((1, D_MODEL), const)
    return pl.pallas_call(
        _merge_kernel,
        grid=(t // MERGE_TM,),
        in_specs=[pl.BlockSpec((MERGE_TM, A_V), row),
                  pl.BlockSpec((MERGE_TM, B_WIDTH), row),
                  pl.BlockSpec((MERGE_TM, D_MODEL), lambda i: (i, 7)),
                  pl.BlockSpec((MERGE_TM, D_MODEL), lambda i: (i, 8)),
                  pl.BlockSpec((MERGE_TM, D_MODEL), row),
                  wspec, wspec, wspec, vspec, vspec],
        out_specs=pl.BlockSpec((MERGE_TM, D_MODEL), row),
        out_shape=jax.ShapeDtypeStruct((t, D_MODEL), F32),
        compiler_params=pltpu.CompilerParams(dimension_semantics=("arbitrary",),
                                             vmem_limit_bytes=VMEM_LIMIT),
        name="merge_ln",
    )(ha, hb, p, p, x, wa, wb, wo, g, b)


def _ffn_kernel(x_ref, wu_ref, wd_ref, g_ref, b_ref, out_ref):
    x = x_ref[...]
    xb = x.astype(BF16)
    acc = None
    for j in range(D_FF // FFN_SUB):
        gt = _dot(xb, wu_ref[:, j * FFN_SUB:(j + 1) * FFN_SUB])
        up = _dot(xb, wu_ref[:, D_FF + j * FFN_SUB:D_FF + (j + 1) * FFN_SUB])
        hid = (gt * jax.nn.sigmoid(gt) * up).astype(BF16)
        part = _dot(hid, wd_ref[j * FFN_SUB:(j + 1) * FFN_SUB, :])
        acc = part if acc is None else acc + part
    out_ref[...] = _layernorm(DEEPNORM_ALPHA * x + acc, g_ref[...], b_ref[...])


def _ffn(x, wu, wd, g, b):
    t = x.shape[0]
    row = lambda i: (i, 0)
    const = lambda i: (0, 0)
    return pl.pallas_call(
        _ffn_kernel,
        grid=(t // FFN_TM,),
        in_specs=[pl.BlockSpec((FFN_TM, D_MODEL), row),
                  pl.BlockSpec((D_MODEL, 2 * D_FF), const, pipeline_mode=pl.Buffered(1)),
                  pl.BlockSpec((D_FF, D_MODEL), const, pipeline_mode=pl.Buffered(1)),
                  pl.BlockSpec((1, D_MODEL), const),
                  pl.BlockSpec((1, D_MODEL), const)],
        out_specs=pl.BlockSpec((FFN_TM, D_MODEL), row),
        out_shape=jax.ShapeDtypeStruct((t, D_MODEL), F32),
        compiler_params=pltpu.CompilerParams(dimension_semantics=("arbitrary",),
                                             vmem_limit_bytes=VMEM_LIMIT),
        name="ffn_ln",
    )(x, wu, wd, g, b)


def _gate_vec(parts, width):
    out = jnp.zeros((DEPTH, width), F32)
    for off, val in parts:
        out = out.at[:, off:off + val.shape[1]].set(val.astype(F32))
    return out


def kernel(x, w_in, b_igate, b_fgate, g_mlstm_norm, conv_w, a_log, dt_bias, g_gdn_norm, w_branch_a, w_branch_b, w_out, ln1_g, ln1_b, w_ffn_up, w_ffn_down, ln2_g, ln2_b):
    batch, seq, _ = x.shape
    t = batch * seq

    o_q, o_if, o_qkvb, o_beta, o_gate = 0, 2 * A_QK + 2 * A_V, 2 * A_QK + 2 * A_V + 2 * A_HEADS, None, None
    o_beta = o_qkvb + 4 * B_WIDTH
    o_gate = o_beta + 2 * B_HEADS
    w_big = jnp.concatenate([w_in[:, :, o_q:o_if], w_in[:, :, o_qkvb:o_beta], w_in[:, :, o_gate:]], axis=-1).astype(BF16)
    w_gates = jnp.concatenate([w_in[:, :, o_if:o_qkvb], w_in[:, :, o_beta:o_gate]], axis=-1)
    n_g = w_gates.shape[-1]
    w_small = jnp.pad(w_gates, ((0, 0), (0, 0), (0, GATE_LANES - n_g))).astype(BF16)
    w_small_t = jnp.pad(jnp.swapaxes(w_gates, 1, 2), ((0, 0), (0, GATE_ROWS - n_g), (0, 0))).astype(BF16)

    bias_row = _gate_vec([(LANE_I, b_igate), (LANE_F, b_fgate), (LANE_A, dt_bias)], GATE_LANES)
    bias_col = _gate_vec([(LANE_I, b_igate), (LANE_F, b_fgate), (LANE_A, dt_bias)], GATE_ROWS)
    alog_row = _gate_vec([(LANE_A, a_log)], GATE_LANES)
    alog_col = _gate_vec([(LANE_A, a_log)], GATE_ROWS)

    wa = w_branch_a.astype(BF16)
    wb = w_branch_b.astype(BF16)
    wo = w_out.astype(BF16)
    wu = w_ffn_up.astype(BF16)
    wd = w_ffn_down.astype(BF16)

    xt = x.reshape(t, D_MODEL)
    for l in range(DEPTH):
        p = _proj(xt, w_big[l])
        ps, pst = _gate_proj(xt, w_small[l], w_small_t[l])
        brow = bias_row[l][None, :]
        bcol = bias_col[l][:, None]
        ha = _mlstm(p, ps, pst, brow, bcol, g_mlstm_norm[l][None, :].astype(F32), batch, seq)
        hb = _gdn(p, ps, pst, conv_w[l].astype(F32), brow, bcol, alog_row[l][None, :], alog_col[l][:, None],
                  g_gdn_norm[l][None, :].astype(F32), batch, seq)
        x1 = _merge(ha, hb, p, xt, wa[l], wb[l], wo[l], ln1_g[l][None, :], ln1_b[l][None, :])
        xt = _ffn(x1, wu[l], wd[l], ln2_g[l][None, :], ln2_b[l][None, :])
    return xt.reshape(batch, seq, D_MODEL)
```

```python
import functools

import jax
import jax.numpy as jnp
from jax import lax
from jax.experimental import pallas as pl
from jax.experimental.pallas import tpu as pltpu

F32 = jnp.float32
BF16 = jnp.bfloat16

D_MODEL = 1024
DEPTH = 4
A_HEADS = 4
A_DK = 128
A_DV = 256
A_QK = A_HEADS * A_DK
A_V = A_HEADS * A_DV
GATE_SOFT_CAP = 15.0
B_HEADS = 8
B_DK = 128
B_DV = 128
B_WIDTH = B_HEADS * B_DK
CONV_K = 4
CHUNK = 64
D_FF = 2816
DEEPNORM_ALPHA = (2.0 * DEPTH) ** 0.25
LN_EPS = 1e-5
RMS_EPS = 1e-6

W_WIDTH = 9216
P_WIDTH = 6144
P_COL_QA, P_COL_KA, P_COL_VA, P_COL_OA, P_COL_ZB, P_COL_GA, P_COL_GB = 0, 512, 1024, 2048, 3072, 4096, 5120
C_WIDTH = 3072
C_COL_QB, C_COL_KB, C_COL_VB = 0, 1024, 2048
GATE_LANES = 128
GATE_ROWS = 32
LANE_I, LANE_F, LANE_BETA, LANE_A = 0, 4, 8, 16
HEAD_GROUP = 4
STACK = HEAD_GROUP * CHUNK

VMEM_LIMIT = 56 * 1024 * 1024


def _dot(a, b):
    return jnp.dot(a, b, preferred_element_type=F32)


def _dot_nt(a, b):
    return lax.dot_general(a, b, (((1,), (1,)), ((), ())), preferred_element_type=F32)


def _dot_tn(a, b):
    return lax.dot_general(a, b, (((0,), (0,)), ((), ())), preferred_element_type=F32)


def _split2(x):
    hi = x.astype(BF16)
    lo = (x - hi.astype(F32)).astype(BF16)
    return hi, lo


def _split3(x):
    x1 = x.astype(BF16)
    r1 = x - x1.astype(F32)
    x2 = r1.astype(BF16)
    x3 = (r1 - x2.astype(F32)).astype(BF16)
    return x1, x2, x3


def _dot3(a, b):
    ah, al = _split2(a)
    bh, bl = _split2(b)
    return _dot(ah, bh) + (_dot(ah, bl) + _dot(al, bh))


def _dot_sel_right(x, m):
    x1, x2, x3 = _split3(x)
    return _dot(x1, m) + (_dot(x2, m) + _dot(x3, m))


def _dot_sel_left(m, x):
    x1, x2, x3 = _split3(x)
    return _dot(m, x1) + (_dot(m, x2) + _dot(m, x3))


def _soft_cap(x):
    return GATE_SOFT_CAP * jnp.tanh(x / GATE_SOFT_CAP)


def _log_sigmoid(x):
    return jnp.minimum(x, 0.0) - jnp.log1p(jnp.exp(-jnp.abs(x)))


def _softplus(x):
    return jnp.maximum(x, 0.0) + jnp.log1p(jnp.exp(-jnp.abs(x)))


def _chunk_constants():
    r = lax.broadcasted_iota(jnp.int32, (CHUNK, CHUNK), 0)
    c = lax.broadcasted_iota(jnp.int32, (CHUNK, CHUNK), 1)
    lower = (c <= r).astype(BF16)
    r2 = lax.broadcasted_iota(jnp.int32, (CHUNK, 2 * CHUNK), 0)
    c2 = lax.broadcasted_iota(jnp.int32, (CHUNK, 2 * CHUNK), 1) & (CHUNK - 1)
    upper2 = (r2 <= c2).astype(BF16)
    eye2 = (r2 == c2).astype(BF16)
    return lower, upper2, eye2


def _stack_masks():
    r = lax.broadcasted_iota(jnp.int32, (STACK, STACK), 0)
    c = lax.broadcasted_iota(jnp.int32, (STACK, STACK), 1)
    same = (r >> 6) == (c >> 6)
    return same & (c <= r), same & (c < r)


def _pair_rows(rows):
    lane = lax.broadcasted_iota(jnp.int32, (1, 2 * CHUNK), 1)
    first = lane < CHUNK
    return jnp.concatenate([jnp.where(first, rows[0], rows[1]), jnp.where(first, rows[2], rows[3])], axis=1)


PROJ_TM = 512
PROJ_TN = 3072
PROJ_SUB = 512
PROJ_PIECE = 256
CONV_ROWS = 256
CONV_TURN = 2
CONV_COLS = B_WIDTH
CONV_PAD = 8


def _interleave(*streams):
    live = [s if isinstance(s, tuple) else (s, 1) for s in streams if s is not None]
    while live:
        for entry in list(live):
            try:
                for _ in range(entry[1]):
                    next(entry[0])
            except StopIteration:
                live.remove(entry)


def _proj_kernel(x_ref, w_ref, cw_ref, o_ref, c_ref, pad_s, hist_s, *, tiles_per_seq):
    j = pl.program_id(0)
    i = pl.program_id(1)
    xb = x_ref[...].astype(BF16)

    @pl.when(i % tiles_per_seq == 0)
    def _():
        hist_s[j] = jnp.zeros((CONV_PAD, CONV_COLS), F32)

    norm_on = jnp.where(j < 2, 1.0, 0.0).astype(F32)
    q_scale = jnp.where(j == 0, B_DK ** -0.5, 1.0).astype(F32)

    def dot_stream(pieces):
        for pc in pieces:
            cols = slice(pc * PROJ_PIECE, (pc + 1) * PROJ_PIECE)
            y = _dot(xb, w_ref[:, cols])
            if pc * PROJ_PIECE < CONV_COLS:
                half, lanes = divmod(pc * PROJ_PIECE, PROJ_SUB)
                lanes = slice(lanes, lanes + PROJ_PIECE)
                pad_s[half, 0:CONV_PAD, lanes] = hist_s[j, :, cols]
                pad_s[half, CONV_PAD:CONV_PAD + PROJ_TM, lanes] = y
                hist_s[j, :, cols] = y[PROJ_TM - CONV_PAD:PROJ_TM, :]
            else:
                o_ref[:, pc * PROJ_PIECE - CONV_COLS:(pc + 1) * PROJ_PIECE - CONV_COLS] = y
            yield

    def conv_stream(half):
        for g in range(PROJ_SUB // B_DK):
            lanes = slice(g * B_DK, (g + 1) * B_DK)
            cols = slice(half * PROJ_SUB + g * B_DK, half * PROJ_SUB + (g + 1) * B_DK)
            for r in range(0, PROJ_TM, CONV_ROWS):
                acc = None
                for kk in range(CONV_K):
                    r0 = CONV_PAD - (CONV_K - 1) + kk + r
                    term = pad_s[half, r0:r0 + CONV_ROWS, lanes] * cw_ref[0, kk:kk + 1, cols]
                    acc = term if acc is None else acc + term
                a = acc * jax.nn.sigmoid(acc)
                rs = lax.rsqrt(jnp.sum(a * a, axis=-1, keepdims=True) + RMS_EPS) * q_scale
                c_ref[r:r + CONV_ROWS, cols] = a * (rs * norm_on + (1.0 - norm_on))
                yield

    per_half = PROJ_SUB // PROJ_PIECE
    n_conv = CONV_COLS // PROJ_PIECE
    n_all = PROJ_TN // PROJ_PIECE
    _interleave(dot_stream(range(0, per_half)))
    _interleave((conv_stream(0), CONV_TURN), dot_stream(range(per_half, n_conv + 2)))
    _interleave((conv_stream(1), CONV_TURN), dot_stream(range(n_conv + 2, n_conv + 6)))
    _interleave(dot_stream(range(n_conv + 6, n_all)))


def _proj(x, w, cw, seq):
    t = x.shape[0]
    n_blk = W_WIDTH // PROJ_TN
    return pl.pallas_call(
        functools.partial(_proj_kernel, tiles_per_seq=seq // PROJ_TM),
        grid=(n_blk, t // PROJ_TM),
        in_specs=[pl.BlockSpec((PROJ_TM, D_MODEL), lambda j, i: (i, 0)),
                  pl.BlockSpec((D_MODEL, PROJ_TN), lambda j, i: (0, j)),
                  pl.BlockSpec((1, CONV_K, CONV_COLS), lambda j, i: (j, 0, 0))],
        out_specs=[pl.BlockSpec((PROJ_TM, PROJ_TN - CONV_COLS), lambda j, i: (i, j)),
                   pl.BlockSpec((PROJ_TM, CONV_COLS), lambda j, i: (i, j))],
        out_shape=[jax.ShapeDtypeStruct((t, P_WIDTH), F32),
                   jax.ShapeDtypeStruct((t, C_WIDTH), F32)],
        scratch_shapes=[pltpu.VMEM((CONV_COLS // PROJ_SUB, CONV_PAD + PROJ_TM, PROJ_SUB), F32),
                        pltpu.VMEM((n_blk, CONV_PAD, CONV_COLS), F32)],
        compiler_params=pltpu.CompilerParams(dimension_semantics=("arbitrary", "arbitrary"),
                                             vmem_limit_bytes=VMEM_LIMIT),
        name="in_proj",
    )(x, w, cw)


def _gate_proj_kernel(x_ref, ws_ref, wst_ref, ps_ref, pst_ref):
    xb = x_ref[...].astype(BF16)
    ps_ref[...] = _dot(xb, ws_ref[...])
    for c in range(PROJ_TM // CHUNK):
        pst_ref[c] = _dot_nt(wst_ref[...], xb[c * CHUNK:(c + 1) * CHUNK])


def _gate_proj(x, ws, wst):
    t = x.shape[0]
    return pl.pallas_call(
        _gate_proj_kernel,
        grid=(t // PROJ_TM,),
        in_specs=[pl.BlockSpec((PROJ_TM, D_MODEL), lambda i: (i, 0)),
                  pl.BlockSpec((D_MODEL, GATE_LANES), lambda i: (0, 0)),
                  pl.BlockSpec((GATE_ROWS, D_MODEL), lambda i: (0, 0))],
        out_specs=[pl.BlockSpec((PROJ_TM, GATE_LANES), lambda i: (i, 0)),
                   pl.BlockSpec((PROJ_TM // CHUNK, GATE_ROWS, CHUNK), lambda i: (i, 0, 0))],
        out_shape=[jax.ShapeDtypeStruct((t, GATE_LANES), F32),
                   jax.ShapeDtypeStruct((t // CHUNK, GATE_ROWS, CHUNK), F32)],
        compiler_params=pltpu.CompilerParams(dimension_semantics=("arbitrary",),
                                             vmem_limit_bytes=VMEM_LIMIT),
        name="gate_proj",
    )(x, ws, wst)


MLSTM_TB = 512


def _mlstm_kernel(q_ref, k_ref, v_ref, o_ref, ps_ref, pst_ref, brow_ref, bcol_ref, gn_ref,
                  out_ref, c_s, n_s, m_s, nloc_s, rsum_s, mloc_s, gcol_s, kv_s, ksum_s, mdec_s, gend_s):
    @pl.when(pl.program_id(1) == 0)
    def _():
        c_s[...] = jnp.zeros_like(c_s)
        n_s[...] = jnp.zeros_like(n_s)
        m_s[...] = jnp.zeros_like(m_s)

    lower, upper2, eye2 = _chunk_constants()
    mask_incl, _ = _stack_masks()
    brow = brow_ref[...]
    bcol = bcol_ref[...]
    scale = A_DK ** -0.5

    n_chunks = MLSTM_TB // CHUNK

    def local(chunks):
        for c in chunks:
            rows = slice(c * CHUNK, (c + 1) * CHUNK)
            cap_c = _soft_cap(ps_ref[rows, :] + brow)
            gc = _dot_sel_left(lower, _log_sigmoid(cap_c))
            cap_r = _soft_cap(pst_ref[c] + bcol)
            gr2 = _dot_sel_right(_log_sigmoid(cap_r), upper2)
            li2 = _dot_sel_right(cap_r, eye2)
            row_all = _pair_rows([gr2[LANE_F + h:LANE_F + h + 1] - li2[LANE_I + h:LANE_I + h + 1]
                                  for h in range(A_HEADS)])
            gcol, qh, kh, vh = [], [], [], []
            for h in range(A_HEADS):
                gcol.append(gc[:, LANE_F + h:LANE_F + h + 1])
                qh.append(q_ref[rows, h * A_DK:(h + 1) * A_DK] * scale)
                kh.append(k_ref[rows, h * A_DK:(h + 1) * A_DK])
                vh.append(v_ref[rows, h * A_DV:(h + 1) * A_DV])
            dmat = jnp.concatenate([gcol[h] - row_all for h in range(A_HEADS)], axis=0)
            dmat = jnp.where(mask_incl, dmat, -jnp.inf)
            m_loc = jnp.max(dmat, axis=-1, keepdims=True)
            qs = jnp.concatenate(qh, axis=0).astype(BF16)
            ks = jnp.concatenate(kh, axis=0).astype(BF16)
            vs = jnp.concatenate(vh, axis=0).astype(BF16)
            s = _dot_nt(qs, ks) * jnp.exp(dmat - m_loc)
            nloc_s[c] = _dot(s.astype(BF16), vs)
            rsum_s[c] = jnp.sum(s, axis=-1, keepdims=True)
            mloc_s[c] = m_loc
            gcol_s[c] = jnp.concatenate(gcol, axis=0)
            for h in range(A_HEADS):
                g_end = gcol[h][CHUNK - 1:CHUNK, :]
                dec = g_end - gcol[h] + cap_c[:, LANE_I + h:LANE_I + h + 1]
                m_dec = jnp.max(dec, axis=0, keepdims=True)
                kw = jnp.exp(dec - m_dec) * kh[h]
                kv_s[c, h] = _dot_tn(kw.astype(BF16), vh[h].astype(BF16))
                ksum_s[c, h] = jnp.sum(kw, axis=0, keepdims=True)
                mdec_s[c, h] = m_dec
                gend_s[c, h] = g_end
            yield

    def recur(chunks):
        for c in chunks:
            rows = slice(c * CHUNK, (c + 1) * CHUNK)
            gcol = gcol_s[c]
            mst = [m_s[h] for h in range(A_HEADS)]
            qh = [q_ref[rows, h * A_DK:(h + 1) * A_DK] * scale for h in range(A_HEADS)]
            inter = jnp.concatenate([gcol[h * CHUNK:(h + 1) * CHUNK] + mst[h] for h in range(A_HEADS)], axis=0)
            m_loc = mloc_s[c]
            m_t = jnp.maximum(inter, m_loc)
            a = jnp.exp(m_loc - m_t)
            sc = jnp.exp(inter - m_t)
            q_c = jnp.concatenate([_dot(qh[h].astype(BF16), c_s[h].astype(BF16)) for h in range(A_HEADS)], axis=0)
            q_n = jnp.concatenate([jnp.sum(qh[h] * n_s[h], axis=-1, keepdims=True) for h in range(A_HEADS)], axis=0)
            for h in range(A_HEADS):
                g_end = gend_s[c, h]
                m_new = jnp.maximum(g_end + mst[h], mdec_s[c, h])
                carry_scale = jnp.exp(g_end + mst[h] - m_new)
                local_scale = jnp.exp(mdec_s[c, h] - m_new)
                c_s[h] = carry_scale * c_s[h] + local_scale * kv_s[c, h]
                n_s[h] = carry_scale * n_s[h] + local_scale * ksum_s[c, h]
                m_s[h] = m_new
            num = a * nloc_s[c] + sc * q_c
            den = a * rsum_s[c] + sc * q_n
            hout = num / jnp.maximum(jnp.abs(den), jnp.exp(-m_t))
            hn = hout * lax.rsqrt(jnp.mean(hout * hout, axis=-1, keepdims=True) + RMS_EPS)
            for h in range(A_HEADS):
                cols = slice(h * A_DV, (h + 1) * A_DV)
                out_ref[rows, cols] = (hn[h * CHUNK:(h + 1) * CHUNK] * gn_ref[:, cols]
                                       * jax.nn.sigmoid(o_ref[rows, cols]))
            yield

    _interleave(local(range(0, 1)))
    _interleave(local(range(1, n_chunks)), recur(range(n_chunks)))


def _mlstm(p, ps, pst, brow, bcol, gn, batch, seq):
    t = batch * seq
    nb = seq // MLSTM_TB
    nch = MLSTM_TB // CHUNK
    tok = lambda b, i: b * nb + i
    return pl.pallas_call(
        _mlstm_kernel,
        grid=(batch, nb),
        in_specs=[pl.BlockSpec((MLSTM_TB, A_QK), lambda b, i: (tok(b, i), P_COL_QA // A_QK)),
                  pl.BlockSpec((MLSTM_TB, A_QK), lambda b, i: (tok(b, i), P_COL_KA // A_QK)),
                  pl.BlockSpec((MLSTM_TB, A_V), lambda b, i: (tok(b, i), P_COL_VA // A_V)),
                  pl.BlockSpec((MLSTM_TB, A_V), lambda b, i: (tok(b, i), P_COL_OA // A_V)),
                  pl.BlockSpec((MLSTM_TB, GATE_LANES), lambda b, i: (tok(b, i), 0)),
                  pl.BlockSpec((MLSTM_TB // CHUNK, GATE_ROWS, CHUNK), lambda b, i: (tok(b, i), 0, 0)),
                  pl.BlockSpec((1, GATE_LANES), lambda b, i: (0, 0)),
                  pl.BlockSpec((GATE_ROWS, 1), lambda b, i: (0, 0)),
                  pl.BlockSpec((1, A_V), lambda b, i: (0, 0))],
        out_specs=pl.BlockSpec((MLSTM_TB, A_V), lambda b, i: (tok(b, i), 0)),
        out_shape=jax.ShapeDtypeStruct((t, A_V), F32),
        scratch_shapes=[pltpu.VMEM((A_HEADS, A_DK, A_DV), F32),
                        pltpu.VMEM((A_HEADS, 1, A_DK), F32),
                        pltpu.VMEM((A_HEADS, 1, 1), F32),
                        pltpu.VMEM((nch, STACK, A_DV), F32),
                        pltpu.VMEM((nch, STACK, 1), F32),
                        pltpu.VMEM((nch, STACK, 1), F32),
                        pltpu.VMEM((nch, STACK, 1), F32),
                        pltpu.VMEM((nch, A_HEADS, A_DK, A_DV), F32),
                        pltpu.VMEM((nch, A_HEADS, 1, A_DK), F32),
                        pltpu.VMEM((nch, A_HEADS, 1, 1), F32),
                        pltpu.VMEM((nch, A_HEADS, 1, 1), F32)],
        compiler_params=pltpu.CompilerParams(dimension_semantics=("arbitrary", "arbitrary"),
                                             vmem_limit_bytes=VMEM_LIMIT),
        name="mlstm",
    )(p, p, p, p, ps, pst, brow, bcol, gn)


GDN_TB = 512
GDN_ROUND = 2


def _gdn_kernel(q_ref, k_ref, v_ref, z_ref, ps_ref, pst_ref, brow_ref, bcol_ref, alrow_ref, alcol_ref, gn_ref,
                out_ref, s_s, u_s, attn_s, wq_s, kd_s, ge_s):
    @pl.when(pl.program_id(1) == 0)
    def _():
        s_s[...] = jnp.zeros_like(s_s)

    lower, upper2, _ = _chunk_constants()
    mask_incl, mask_strict = _stack_masks()
    rr = lax.broadcasted_iota(jnp.int32, (STACK, STACK), 0)
    cc = lax.broadcasted_iota(jnp.int32, (STACK, STACK), 1)
    eye = (rr == cc).astype(F32)
    brow = brow_ref[...]
    bcol = bcol_ref[...]
    neg_a_row = -jnp.exp(alrow_ref[...])
    neg_a_col = -jnp.exp(alcol_ref[...])
    gn = gn_ref[...]

    n_grp = B_HEADS // HEAD_GROUP

    def wy_setup(c):
        rows = slice(c * CHUNK, (c + 1) * CHUNK)
        pre_c = ps_ref[rows, :]
        beta_c = jax.nn.sigmoid(pre_c)
        gc = _dot_sel_left(lower, neg_a_row * _softplus(pre_c + brow))
        gl_r = neg_a_col * _softplus(pst_ref[c] + bcol)
        gr2 = _dot_sel_right(gl_r, upper2)
        chains = []
        for grp in range(n_grp):
            heads = range(grp * HEAD_GROUP, (grp + 1) * HEAD_GROUP)
            row_all = _pair_rows([gr2[LANE_A + h:LANE_A + h + 1] for h in heads])
            gcol, beta, qh, kh, vh = [], [], [], [], []
            for h in heads:
                gcol.append(gc[:, LANE_A + h:LANE_A + h + 1])
                beta.append(beta_c[:, LANE_BETA + h:LANE_BETA + h + 1])
                qh.append(q_ref[rows, h * B_DK:(h + 1) * B_DK])
                kh.append(k_ref[rows, h * B_DK:(h + 1) * B_DK])
                vh.append(v_ref[rows, h * B_DV:(h + 1) * B_DV])
            diff = jnp.concatenate([g - row_all for g in gcol], axis=0)
            decay = jnp.where(mask_incl, jnp.exp(jnp.where(mask_incl, diff, 0.0)), 0.0)
            ks = jnp.concatenate(kh, axis=0).astype(BF16)
            qk_kk = _dot_nt(jnp.concatenate(qh + kh, axis=0).astype(BF16), ks)
            kk = qk_kk[STACK:2 * STACK]
            a_low = jnp.concatenate([beta[n] * kk[n * CHUNK:(n + 1) * CHUNK] for n in range(HEAD_GROUP)], axis=0)
            pw = jnp.where(mask_strict, -(a_low * decay), 0.0)
            attn_s[c, grp] = (qk_kk[0:STACK] * decay).astype(BF16)
            rhs = jnp.concatenate(
                [jnp.concatenate([vh[n] * beta[n], kh[n] * (beta[n] * jnp.exp(gcol[n]))], axis=1)
                 for n in range(HEAD_GROUP)], axis=0).astype(BF16)
            for n, h in enumerate(heads):
                g_end = gcol[n][CHUNK - 1:CHUNK, :]
                wq_s[c, h, CHUNK:2 * CHUNK, :] = (qh[n] * jnp.exp(gcol[n])).astype(BF16)
                kd_s[c, h] = (kh[n] * jnp.exp(g_end - gcol[n])).astype(BF16)
                ge_s[c, h] = jnp.broadcast_to(jnp.exp(g_end), (1, B_DV))
            chains.append(dict(c=c, grp=grp, tinv=eye + pw, pw=pw, rhs=rhs))
        return chains

    def wy_solve(chains):
        for ch in chains:
            x_b = ch["pw"].astype(BF16)
            ch["pw"] = _dot(x_b, x_b)
        yield
        for k in range(1, 6):
            for ch in chains:
                pw_b = ch["pw"].astype(BF16)
                if k < 5:
                    r = _dot(jnp.concatenate([ch["tinv"].astype(BF16), pw_b], axis=0), pw_b)
                    ch["tinv"] = ch["tinv"] + r[0:STACK]
                    ch["pw"] = r[STACK:2 * STACK]
                else:
                    ch["tinv"] = ch["tinv"] + _dot(ch["tinv"].astype(BF16), pw_b)
            yield
        for ch in chains:
            c, grp = ch["c"], ch["grp"]
            sol = _dot(ch["tinv"].astype(BF16), ch["rhs"])
            u_s[c, grp] = sol[:, 0:B_DV]
            for n in range(HEAD_GROUP):
                wq_s[c, grp * HEAD_GROUP + n, 0:CHUNK, :] = sol[n * CHUNK:(n + 1) * CHUNK, B_DV:2 * B_DV].astype(BF16)
        yield

    def recurrence(chunks):
        for c in chunks:
            rows = slice(c * CHUNK, (c + 1) * CHUNK)
            for grp in range(n_grp):
                heads = range(grp * HEAD_GROUP, (grp + 1) * HEAD_GROUP)
                u = u_s[c, grp]
                v_new, q_s = [], []
                for n, h in enumerate(heads):
                    r = _dot(wq_s[c, h], s_s[h].astype(BF16))
                    v_new.append(u[n * CHUNK:(n + 1) * CHUNK] - r[0:CHUNK])
                    q_s.append(r[CHUNK:2 * CHUNK])
                v_new_b = jnp.concatenate(v_new, axis=0).astype(BF16)
                for n, h in enumerate(heads):
                    s_s[h] = s_s[h] * ge_s[c, h] + _dot_tn(kd_s[c, h], v_new_b[n * CHUNK:(n + 1) * CHUNK])
                yield
                o = jnp.concatenate(q_s, axis=0) + _dot(attn_s[c, grp], v_new_b)
                on = o * lax.rsqrt(jnp.mean(o * o, axis=-1, keepdims=True) + RMS_EPS) * gn
                for n, h in enumerate(heads):
                    cols = slice(h * B_DV, (h + 1) * B_DV)
                    zz = z_ref[rows, cols]
                    out_ref[rows, cols] = on[n * CHUNK:(n + 1) * CHUNK] * (zz * jax.nn.sigmoid(zz))
                yield

    rounds = [range(c0, c0 + GDN_ROUND) for c0 in range(0, GDN_TB // CHUNK, GDN_ROUND)]
    prev = None
    for rnd in rounds:
        chains = [ch for c in rnd for ch in wy_setup(c)]
        _interleave(wy_solve(chains), recurrence(prev) if prev is not None else None)
        prev = rnd
    _interleave(recurrence(prev))


def _gdn(p, pc, ps, pst, brow, bcol, alrow, alcol, gn, batch, seq):
    t = batch * seq
    nb = seq // GDN_TB
    nch = GDN_TB // CHUNK
    tok = lambda b, i: b * nb + i
    const2 = lambda b, i: (0, 0)
    return pl.pallas_call(
        _gdn_kernel,
        grid=(batch, nb),
        in_specs=[pl.BlockSpec((GDN_TB, B_WIDTH), lambda b, i: (tok(b, i), C_COL_QB // B_WIDTH)),
                  pl.BlockSpec((GDN_TB, B_WIDTH), lambda b, i: (tok(b, i), C_COL_KB // B_WIDTH)),
                  pl.BlockSpec((GDN_TB, B_WIDTH), lambda b, i: (tok(b, i), C_COL_VB // B_WIDTH)),
                  pl.BlockSpec((GDN_TB, B_WIDTH), lambda b, i: (tok(b, i), P_COL_ZB // B_WIDTH)),
                  pl.BlockSpec((GDN_TB, GATE_LANES), lambda b, i: (tok(b, i), 0)),
                  pl.BlockSpec((GDN_TB // CHUNK, GATE_ROWS, CHUNK), lambda b, i: (tok(b, i), 0, 0)),
                  pl.BlockSpec((1, GATE_LANES), const2),
                  pl.BlockSpec((GATE_ROWS, 1), const2),
                  pl.BlockSpec((1, GATE_LANES), const2),
                  pl.BlockSpec((GATE_ROWS, 1), const2),
                  pl.BlockSpec((1, B_DV), const2)],
        out_specs=pl.BlockSpec((GDN_TB, B_WIDTH), lambda b, i: (tok(b, i), 0)),
        out_shape=jax.ShapeDtypeStruct((t, B_WIDTH), F32),
        scratch_shapes=[pltpu.VMEM((B_HEADS, B_DK, B_DV), F32),
                        pltpu.VMEM((nch, B_HEADS // HEAD_GROUP, STACK, B_DV), F32),
                        pltpu.VMEM((nch, B_HEADS // HEAD_GROUP, STACK, STACK), BF16),
                        pltpu.VMEM((nch, B_HEADS, 2 * CHUNK, B_DK), BF16),
                        pltpu.VMEM((nch, B_HEADS, CHUNK, B_DK), BF16),
                        pltpu.VMEM((nch, B_HEADS, 1, B_DV), F32)],
        compiler_params=pltpu.CompilerParams(dimension_semantics=("arbitrary", "arbitrary"),
                                             vmem_limit_bytes=VMEM_LIMIT),
        name="gdn",
    )(pc, pc, pc, p, ps, pst, brow, bcol, alrow, alcol, gn)


MERGE_TM = 512
FFN_TM = 512
FFN_SUB = 256


def _layernorm(r, g, b):
    mu = jnp.mean(r, axis=-1, keepdims=True)
    d = r - mu
    var = jnp.mean(d * d, axis=-1, keepdims=True)
    return d * lax.rsqrt(var + LN_EPS) * g + b


def _merge_kernel(ha_ref, hb_ref, ga_ref, gb_ref, x_ref, wa_ref, wb_ref, wo_ref, g_ref, b_ref, out_ref):
    ya = _dot(ha_ref[...].astype(BF16), wa_ref[...])
    yb = _dot(hb_ref[...].astype(BF16), wb_ref[...])
    y = jax.nn.sigmoid(ga_ref[...]) * ya + jax.nn.sigmoid(gb_ref[...]) * yb
    mix = _dot(y.astype(BF16), wo_ref[...])
    out_ref[...] = _layernorm(DEEPNORM_ALPHA * x_ref[...] + mix, g_ref[...], b_ref[...])


def _merge(ha, hb, p, x, wa, wb, wo, g, b):
    t = x.shape[0]
    row = lambda i: (i, 0)
    const = lambda i: (0, 0)
    wspec = pl.BlockSpec((D_MODEL, D_MODEL), const, pipeline_mode=pl.Buffered(1))
    vspec = pl.BlockSpec((1, D_MODEL), const)
    return pl.pallas_call(
        _merge_kernel,
        grid=(t // MERGE_TM,),
        in_specs=[pl.BlockSpec((MERGE_TM, A_V), row),
                  pl.BlockSpec((MERGE_TM, B_WIDTH), row),
                  pl.BlockSpec((MERGE_TM, D_MODEL), lambda i: (i, P_COL_GA // D_MODEL)),
                  pl.BlockSpec((MERGE_TM, D_MODEL), lambda i: (i, P_COL_GB // D_MODEL)),
                  pl.BlockSpec((MERGE_TM, D_MODEL), row),
                  wspec, wspec, wspec, vspec, vspec],
        out_specs=pl.BlockSpec((MERGE_TM, D_MODEL), row),
        out_shape=jax.ShapeDtypeStruct((t, D_MODEL), F32),
        compiler_params=pltpu.CompilerParams(dimension_semantics=("arbitrary",),
                                             vmem_limit_bytes=VMEM_LIMIT),
        name="merge_ln",
    )(ha, hb, p, p, x, wa, wb, wo, g, b)


def _ffn_kernel(x_ref, wu_ref, wd_ref, g_ref, b_ref, out_ref):
    x = x_ref[...]
    xb = x.astype(BF16)
    acc = None
    for j in range(D_FF // FFN_SUB):
        gt = _dot(xb, wu_ref[:, j * FFN_SUB:(j + 1) * FFN_SUB])
        up = _dot(xb, wu_ref[:, D_FF + j * FFN_SUB:D_FF + (j + 1) * FFN_SUB])
        hid = (gt * jax.nn.sigmoid(gt) * up).astype(BF16)
        part = _dot(hid, wd_ref[j * FFN_SUB:(j + 1) * FFN_SUB, :])
        acc = part if acc is None else acc + part
    out_ref[...] = _layernorm(DEEPNORM_ALPHA * x + acc, g_ref[...], b_ref[...])


def _ffn(x, wu, wd, g, b):
    t = x.shape[0]
    row = lambda i: (i, 0)
    const = lambda i: (0, 0)
    return pl.pallas_call(
        _ffn_kernel,
        grid=(t // FFN_TM,),
        in_specs=[pl.BlockSpec((FFN_TM, D_MODEL), row),
                  pl.BlockSpec((D_MODEL, 2 * D_FF), const, pipeline_mode=pl.Buffered(1)),
                  pl.BlockSpec((D_FF, D_MODEL), const, pipeline_mode=pl.Buffered(1)),
                  pl.BlockSpec((1, D_MODEL), const),
                  pl.BlockSpec((1, D_MODEL), const)],
        out_specs=pl.BlockSpec((FFN_TM, D_MODEL), row),
        out_shape=jax.ShapeDtypeStruct((t, D_MODEL), F32),
        compiler_params=pltpu.CompilerParams(dimension_semantics=("arbitrary",),
                                             vmem_limit_bytes=VMEM_LIMIT),
        name="ffn_ln",
    )(x, wu, wd, g, b)


def _gate_vec(parts, width):
    out = jnp.zeros((DEPTH, width), F32)
    for off, val in parts:
        out = out.at[:, off:off + val.shape[1]].set(val.astype(F32))
    return out


def kernel(x, w_in, b_igate, b_fgate, g_mlstm_norm, conv_w, a_log, dt_bias, g_gdn_norm, w_branch_a, w_branch_b, w_out, ln1_g, ln1_b, w_ffn_up, w_ffn_down, ln2_g, ln2_b):
    batch, seq, _ = x.shape
    t = batch * seq

    o_va = 2 * A_QK
    o_oa = o_va + A_V
    o_if = o_oa + A_V
    o_qb = o_if + 2 * A_HEADS
    o_kb, o_vb, o_zb = o_qb + B_WIDTH, o_qb + 2 * B_WIDTH, o_qb + 3 * B_WIDTH
    o_beta = o_zb + B_WIDTH
    o_gate = o_beta + 2 * B_HEADS
    w_big = jnp.concatenate([w_in[:, :, o_qb:o_kb], w_in[:, :, 0:o_oa],
                             w_in[:, :, o_kb:o_vb], w_in[:, :, o_oa:o_if], w_in[:, :, o_zb:o_beta],
                             w_in[:, :, o_vb:o_zb], w_in[:, :, o_gate:]], axis=-1).astype(BF16)
    w_gates = jnp.concatenate([w_in[:, :, o_if:o_qb], w_in[:, :, o_beta:o_gate]], axis=-1)
    conv_blocks = jnp.swapaxes(conv_w.astype(F32).reshape(DEPTH, CONV_K, 3, CONV_COLS), 1, 2)
    n_g = w_gates.shape[-1]
    w_small = jnp.pad(w_gates, ((0, 0), (0, 0), (0, GATE_LANES - n_g))).astype(BF16)
    w_small_t = jnp.pad(jnp.swapaxes(w_gates, 1, 2), ((0, 0), (0, GATE_ROWS - n_g), (0, 0))).astype(BF16)

    bias_row = _gate_vec([(LANE_I, b_igate), (LANE_F, b_fgate), (LANE_A, dt_bias)], GATE_LANES)
    bias_col = _gate_vec([(LANE_I, b_igate), (LANE_F, b_fgate), (LANE_A, dt_bias)], GATE_ROWS)
    alog_row = _gate_vec([(LANE_A, a_log)], GATE_LANES)
    alog_col = _gate_vec([(LANE_A, a_log)], GATE_ROWS)

    wa = w_branch_a.astype(BF16)
    wb = w_branch_b.astype(BF16)
    wo = w_out.astype(BF16)
    wu = w_ffn_up.astype(BF16)
    wd = w_ffn_down.astype(BF16)

    xt = x.reshape(t, D_MODEL)
    for l in range(DEPTH):
        p, pc = _proj(xt, w_big[l], conv_blocks[l], seq)
        ps, pst = _gate_proj(xt, w_small[l], w_small_t[l])
        brow = bias_row[l][None, :]
        bcol = bias_col[l][:, None]
        ha = _mlstm(p, ps, pst, brow, bcol, g_mlstm_norm[l][None, :].astype(F32), batch, seq)
        hb = _gdn(p, pc, ps, pst, brow, bcol, alog_row[l][None, :], alog_col[l][:, None],
                  g_gdn_norm[l][None, :].astype(F32), batch, seq)
        x1 = _merge(ha, hb, p, xt, wa[l], wb[l], wo[l], ln1_g[l][None, :], ln1_b[l][None, :])
        xt = _ffn(x1, wu[l], wd[l], ln2_g[l][None, :], ln2_b[l][None, :])
    return xt.reshape(batch, seq, D_MODEL)
```

```python
import functools

import jax
import jax.numpy as jnp
from jax import lax
from jax.experimental import pallas as pl
from jax.experimental.pallas import tpu as pltpu

F32 = jnp.float32
BF16 = jnp.bfloat16

D_MODEL = 1024
DEPTH = 4
A_HEADS = 4
A_DK = 128
A_DV = 256
A_QK = A_HEADS * A_DK
A_V = A_HEADS * A_DV
GATE_SOFT_CAP = 15.0
B_HEADS = 8
B_DK = 128
B_DV = 128
B_WIDTH = B_HEADS * B_DK
CONV_K = 4
CHUNK = 64
D_FF = 2816
DEEPNORM_ALPHA = (2.0 * DEPTH) ** 0.25
LN_EPS = 1e-5
RMS_EPS = 1e-6

W_WIDTH = 9216
P_WIDTH = 6144
P_COL_QA, P_COL_KA, P_COL_VA, P_COL_OA, P_COL_ZB, P_COL_GA, P_COL_GB = 0, 512, 1024, 2048, 3072, 4096, 5120
C_WIDTH = 3072
C_COL_QB, C_COL_KB, C_COL_VB = 0, 1024, 2048
GATE_LANES = 128
GATE_ROWS = 32
LANE_I, LANE_F, LANE_BETA, LANE_A = 0, 4, 8, 16
HEAD_GROUP = 4
STACK = HEAD_GROUP * CHUNK

VMEM_LIMIT = 56 * 1024 * 1024


def _dot(a, b):
    return jnp.dot(a, b, preferred_element_type=F32)


def _dot_nt(a, b):
    return lax.dot_general(a, b, (((1,), (1,)), ((), ())), preferred_element_type=F32)


def _dot_tn(a, b):
    return lax.dot_general(a, b, (((0,), (0,)), ((), ())), preferred_element_type=F32)


def _split2(x):
    hi = x.astype(BF16)
    lo = (x - hi.astype(F32)).astype(BF16)
    return hi, lo


def _split3(x):
    x1 = x.astype(BF16)
    r1 = x - x1.astype(F32)
    x2 = r1.astype(BF16)
    x3 = (r1 - x2.astype(F32)).astype(BF16)
    return x1, x2, x3


def _dot3(a, b):
    ah, al = _split2(a)
    bh, bl = _split2(b)
    return _dot(ah, bh) + (_dot(ah, bl) + _dot(al, bh))


def _dot_sel_right(x, m):
    x1, x2, x3 = _split3(x)
    return _dot(x1, m) + (_dot(x2, m) + _dot(x3, m))


def _dot_sel_left(m, x):
    x1, x2, x3 = _split3(x)
    return _dot(m, x1) + (_dot(m, x2) + _dot(m, x3))


def _soft_cap(x):
    return GATE_SOFT_CAP * jnp.tanh(x / GATE_SOFT_CAP)


def _log_sigmoid(x):
    return jnp.minimum(x, 0.0) - jnp.log1p(jnp.exp(-jnp.abs(x)))


def _softplus(x):
    return jnp.maximum(x, 0.0) + jnp.log1p(jnp.exp(-jnp.abs(x)))


def _chunk_constants():
    r = lax.broadcasted_iota(jnp.int32, (CHUNK, CHUNK), 0)
    c = lax.broadcasted_iota(jnp.int32, (CHUNK, CHUNK), 1)
    lower = (c <= r).astype(BF16)
    r2 = lax.broadcasted_iota(jnp.int32, (CHUNK, 2 * CHUNK), 0)
    c2 = lax.broadcasted_iota(jnp.int32, (CHUNK, 2 * CHUNK), 1) & (CHUNK - 1)
    upper2 = (r2 <= c2).astype(BF16)
    eye2 = (r2 == c2).astype(BF16)
    return lower, upper2, eye2


def _stack_masks():
    r = lax.broadcasted_iota(jnp.int32, (STACK, STACK), 0)
    c = lax.broadcasted_iota(jnp.int32, (STACK, STACK), 1)
    same = (r >> 6) == (c >> 6)
    return same & (c <= r), same & (c < r)


def _pair_rows(rows):
    lane = lax.broadcasted_iota(jnp.int32, (1, 2 * CHUNK), 1)
    first = lane < CHUNK
    return jnp.concatenate([jnp.where(first, rows[0], rows[1]), jnp.where(first, rows[2], rows[3])], axis=1)


PROJ_TM = 512
PROJ_TN = 3072
PROJ_SUB = 512
PROJ_PIECE = 256
CONV_ROWS = 256
CONV_TURN = 2
CONV_COLS = B_WIDTH
CONV_PAD = 8


def _interleave(*streams):
    live = [s if isinstance(s, tuple) else (s, 1) for s in streams if s is not None]
    while live:
        for entry in list(live):
            try:
                for _ in range(entry[1]):
                    next(entry[0])
            except StopIteration:
                live.remove(entry)


def _proj_kernel(x_ref, w_ref, cw_ref, o_ref, c_ref, pad_s, hist_s, *, tiles_per_seq):
    j = pl.program_id(0)
    i = pl.program_id(1)
    xb = x_ref[...].astype(BF16)

    @pl.when(i % tiles_per_seq == 0)
    def _():
        hist_s[j] = jnp.zeros((CONV_PAD, CONV_COLS), F32)

    norm_on = jnp.where(j < 2, 1.0, 0.0).astype(F32)
    q_scale = jnp.where(j == 0, B_DK ** -0.5, 1.0).astype(F32)

    def dot_stream(pieces):
        for pc in pieces:
            cols = slice(pc * PROJ_PIECE, (pc + 1) * PROJ_PIECE)
            y = _dot(xb, w_ref[:, cols])
            if pc * PROJ_PIECE < CONV_COLS:
                half, lanes = divmod(pc * PROJ_PIECE, PROJ_SUB)
                lanes = slice(lanes, lanes + PROJ_PIECE)
                pad_s[half, 0:CONV_PAD, lanes] = hist_s[j, :, cols]
                pad_s[half, CONV_PAD:CONV_PAD + PROJ_TM, lanes] = y
                hist_s[j, :, cols] = y[PROJ_TM - CONV_PAD:PROJ_TM, :]
            else:
                o_ref[:, pc * PROJ_PIECE - CONV_COLS:(pc + 1) * PROJ_PIECE - CONV_COLS] = y
            yield

    def conv_stream(half):
        for g in range(PROJ_SUB // B_DK):
            lanes = slice(g * B_DK, (g + 1) * B_DK)
            cols = slice(half * PROJ_SUB + g * B_DK, half * PROJ_SUB + (g + 1) * B_DK)
            for r in range(0, PROJ_TM, CONV_ROWS):
                acc = None
                for kk in range(CONV_K):
                    r0 = CONV_PAD - (CONV_K - 1) + kk + r
                    term = pad_s[half, r0:r0 + CONV_ROWS, lanes] * cw_ref[0, kk:kk + 1, cols]
                    acc = term if acc is None else acc + term
                a = acc * jax.nn.sigmoid(acc)
                rs = lax.rsqrt(jnp.sum(a * a, axis=-1, keepdims=True) + RMS_EPS) * q_scale
                c_ref[r:r + CONV_ROWS, cols] = a * (rs * norm_on + (1.0 - norm_on))
                yield

    per_half = PROJ_SUB // PROJ_PIECE
    n_conv = CONV_COLS // PROJ_PIECE
    n_all = PROJ_TN // PROJ_PIECE
    _interleave(dot_stream(range(0, per_half)))
    _interleave((conv_stream(0), CONV_TURN), dot_stream(range(per_half, n_conv + 2)))
    _interleave((conv_stream(1), CONV_TURN), dot_stream(range(n_conv + 2, n_conv + 6)))
    _interleave(dot_stream(range(n_conv + 6, n_all)))


def _proj(x, w, cw, seq):
    t = x.shape[0]
    n_blk = W_WIDTH // PROJ_TN
    return pl.pallas_call(
        functools.partial(_proj_kernel, tiles_per_seq=seq // PROJ_TM),
        grid=(n_blk, t // PROJ_TM),
        in_specs=[pl.BlockSpec((PROJ_TM, D_MODEL), lambda j, i: (i, 0)),
                  pl.BlockSpec((D_MODEL, PROJ_TN), lambda j, i: (0, j)),
                  pl.BlockSpec((1, CONV_K, CONV_COLS), lambda j, i: (j, 0, 0))],
        out_specs=[pl.BlockSpec((PROJ_TM, PROJ_TN - CONV_COLS), lambda j, i: (i, j)),
                   pl.BlockSpec((PROJ_TM, CONV_COLS), lambda j, i: (i, j))],
        out_shape=[jax.ShapeDtypeStruct((t, P_WIDTH), F32),
                   jax.ShapeDtypeStruct((t, C_WIDTH), F32)],
        scratch_shapes=[pltpu.VMEM((CONV_COLS // PROJ_SUB, CONV_PAD + PROJ_TM, PROJ_SUB), F32),
                        pltpu.VMEM((n_blk, CONV_PAD, CONV_COLS), F32)],
        compiler_params=pltpu.CompilerParams(dimension_semantics=("arbitrary", "arbitrary"),
                                             vmem_limit_bytes=VMEM_LIMIT),
        name="in_proj",
    )(x, w, cw)


def _gate_proj_kernel(x_ref, ws_ref, wst_ref, ps_ref, pst_ref):
    xb = x_ref[...].astype(BF16)
    ps_ref[...] = _dot(xb, ws_ref[...])
    for c in range(PROJ_TM // CHUNK):
        pst_ref[c] = _dot_nt(wst_ref[...], xb[c * CHUNK:(c + 1) * CHUNK])


def _gate_proj(x, ws, wst):
    t = x.shape[0]
    return pl.pallas_call(
        _gate_proj_kernel,
        grid=(t // PROJ_TM,),
        in_specs=[pl.BlockSpec((PROJ_TM, D_MODEL), lambda i: (i, 0)),
                  pl.BlockSpec((D_MODEL, GATE_LANES), lambda i: (0, 0)),
                  pl.BlockSpec((GATE_ROWS, D_MODEL), lambda i: (0, 0))],
        out_specs=[pl.BlockSpec((PROJ_TM, GATE_LANES), lambda i: (i, 0)),
                   pl.BlockSpec((PROJ_TM // CHUNK, GATE_ROWS, CHUNK), lambda i: (i, 0, 0))],
        out_shape=[jax.ShapeDtypeStruct((t, GATE_LANES), F32),
                   jax.ShapeDtypeStruct((t // CHUNK, GATE_ROWS, CHUNK), F32)],
        compiler_params=pltpu.CompilerParams(dimension_semantics=("arbitrary",),
                                             vmem_limit_bytes=VMEM_LIMIT),
        name="gate_proj",
    )(x, ws, wst)


MLSTM_TB = 512


def _mlstm_kernel(q_ref, k_ref, v_ref, o_ref, ps_ref, pst_ref, brow_ref, bcol_ref, gn_ref,
                  out_ref, c_s, n_s, m_s, nloc_s, rsum_s, mloc_s, gcol_s, kv_s, ksum_s, mdec_s, gend_s):
    @pl.when(pl.program_id(1) == 0)
    def _():
        c_s[...] = jnp.zeros_like(c_s)
        n_s[...] = jnp.zeros_like(n_s)
        m_s[...] = jnp.zeros_like(m_s)

    lower, upper2, eye2 = _chunk_constants()
    incl = (lax.broadcasted_iota(jnp.int32, (CHUNK, CHUNK), 1)
            <= lax.broadcasted_iota(jnp.int32, (CHUNK, CHUNK), 0))
    brow = brow_ref[...]
    bcol = bcol_ref[...]
    scale = A_DK ** -0.5

    n_chunks = MLSTM_TB // CHUNK

    def local(chunks):
        for c in chunks:
            rows = slice(c * CHUNK, (c + 1) * CHUNK)
            cap_c = _soft_cap(ps_ref[rows, :] + brow)
            gc = _dot_sel_left(lower, _log_sigmoid(cap_c))
            cap_r = _soft_cap(pst_ref[c] + bcol)
            gr2 = _dot_sel_right(_log_sigmoid(cap_r), upper2)
            li2 = _dot_sel_right(cap_r, eye2)
            for h in range(A_HEADS):
                gcol = gc[:, LANE_F + h:LANE_F + h + 1]
                row = (gr2[LANE_F + h:LANE_F + h + 1] - li2[LANE_I + h:LANE_I + h + 1])[:, 0:CHUNK]
                q = q_ref[rows, h * A_DK:(h + 1) * A_DK] * scale
                k = k_ref[rows, h * A_DK:(h + 1) * A_DK]
                v = v_ref[rows, h * A_DV:(h + 1) * A_DV].astype(BF16)
                dmat = jnp.where(incl, gcol - row, -jnp.inf)
                m_loc = jnp.max(dmat, axis=-1, keepdims=True)
                s = _dot_nt(q.astype(BF16), k.astype(BF16)) * jnp.exp(dmat - m_loc)
                nloc_s[c, h] = _dot(s.astype(BF16), v)
                rsum_s[c, h] = jnp.sum(s, axis=-1, keepdims=True)
                mloc_s[c, h] = m_loc
                gcol_s[c, h] = gcol
                g_end = gcol[CHUNK - 1:CHUNK, :]
                dec = g_end - gcol + cap_c[:, LANE_I + h:LANE_I + h + 1]
                m_dec = jnp.max(dec, axis=0, keepdims=True)
                kw = jnp.exp(dec - m_dec) * k
                kv_s[c, h] = _dot_tn(kw.astype(BF16), v)
                ksum_s[c, h] = jnp.sum(kw, axis=0, keepdims=True)
                mdec_s[c, h] = m_dec
                gend_s[c, h] = g_end
            yield

    def recur(chunks):
        for c in chunks:
            rows = slice(c * CHUNK, (c + 1) * CHUNK)
            for h in range(A_HEADS):
                cols = slice(h * A_DV, (h + 1) * A_DV)
                mst = m_s[h]
                q = q_ref[rows, h * A_DK:(h + 1) * A_DK] * scale
                inter = gcol_s[c, h] + mst
                m_loc = mloc_s[c, h]
                m_t = jnp.maximum(inter, m_loc)
                q_c = _dot(q.astype(BF16), c_s[h].astype(BF16))
                q_n = jnp.sum(q * n_s[h], axis=-1, keepdims=True)
                g_end = gend_s[c, h]
                m_new = jnp.maximum(g_end + mst, mdec_s[c, h])
                carry_scale = jnp.exp(g_end + mst - m_new)
                local_scale = jnp.exp(mdec_s[c, h] - m_new)
                c_s[h] = carry_scale * c_s[h] + local_scale * kv_s[c, h]
                n_s[h] = carry_scale * n_s[h] + local_scale * ksum_s[c, h]
                m_s[h] = m_new
                a = jnp.exp(m_loc - m_t)
                sc = jnp.exp(inter - m_t)
                num = a * nloc_s[c, h] + sc * q_c
                den = a * rsum_s[c, h] + sc * q_n
                hout = num * (1.0 / jnp.maximum(jnp.abs(den), jnp.exp(-m_t)))
                hn = hout * lax.rsqrt(jnp.mean(hout * hout, axis=-1, keepdims=True) + RMS_EPS)
                out_ref[rows, cols] = hn * gn_ref[:, cols] * jax.nn.sigmoid(o_ref[rows, cols])
            yield

    _interleave(local(range(0, 1)))
    _interleave(local(range(1, n_chunks)), recur(range(n_chunks)))


def _mlstm(p, ps, pst, brow, bcol, gn, batch, seq):
    t = batch * seq
    nb = seq // MLSTM_TB
    nch = MLSTM_TB // CHUNK
    tok = lambda b, i: b * nb + i
    return pl.pallas_call(
        _mlstm_kernel,
        grid=(batch, nb),
        in_specs=[pl.BlockSpec((MLSTM_TB, A_QK), lambda b, i: (tok(b, i), P_COL_QA // A_QK)),
                  pl.BlockSpec((MLSTM_TB, A_QK), lambda b, i: (tok(b, i), P_COL_KA // A_QK)),
                  pl.BlockSpec((MLSTM_TB, A_V), lambda b, i: (tok(b, i), P_COL_VA // A_V)),
                  pl.BlockSpec((MLSTM_TB, A_V), lambda b, i: (tok(b, i), P_COL_OA // A_V)),
                  pl.BlockSpec((MLSTM_TB, GATE_LANES), lambda b, i: (tok(b, i), 0)),
                  pl.BlockSpec((MLSTM_TB // CHUNK, GATE_ROWS, CHUNK), lambda b, i: (tok(b, i), 0, 0)),
                  pl.BlockSpec((1, GATE_LANES), lambda b, i: (0, 0)),
                  pl.BlockSpec((GATE_ROWS, 1), lambda b, i: (0, 0)),
                  pl.BlockSpec((1, A_V), lambda b, i: (0, 0))],
        out_specs=pl.BlockSpec((MLSTM_TB, A_V), lambda b, i: (tok(b, i), 0)),
        out_shape=jax.ShapeDtypeStruct((t, A_V), F32),
        scratch_shapes=[pltpu.VMEM((A_HEADS, A_DK, A_DV), F32),
                        pltpu.VMEM((A_HEADS, 1, A_DK), F32),
                        pltpu.VMEM((A_HEADS, 1, 1), F32),
                        pltpu.VMEM((nch, A_HEADS, CHUNK, A_DV), F32),
                        pltpu.VMEM((nch, A_HEADS, CHUNK, 1), F32),
                        pltpu.VMEM((nch, A_HEADS, CHUNK, 1), F32),
                        pltpu.VMEM((nch, A_HEADS, CHUNK, 1), F32),
                        pltpu.VMEM((nch, A_HEADS, A_DK, A_DV), F32),
                        pltpu.VMEM((nch, A_HEADS, 1, A_DK), F32),
                        pltpu.VMEM((nch, A_HEADS, 1, 1), F32),
                        pltpu.VMEM((nch, A_HEADS, 1, 1), F32)],
        compiler_params=pltpu.CompilerParams(dimension_semantics=("arbitrary", "arbitrary"),
                                             vmem_limit_bytes=VMEM_LIMIT),
        name="mlstm",
    )(p, p, p, p, ps, pst, brow, bcol, gn)


GDN_TB = 512
GDN_ROUND = 2


def _gdn_kernel(q_ref, k_ref, v_ref, z_ref, ps_ref, pst_ref, brow_ref, bcol_ref, alrow_ref, alcol_ref, gn_ref,
                out_ref, s_s, u_s, attn_s, wq_s, kd_s, ge_s):
    @pl.when(pl.program_id(1) == 0)
    def _():
        s_s[...] = jnp.zeros_like(s_s)

    lower, upper2, _ = _chunk_constants()
    rr = lax.broadcasted_iota(jnp.int32, (CHUNK, CHUNK), 0)
    cc = lax.broadcasted_iota(jnp.int32, (CHUNK, CHUNK), 1)
    incl = cc <= rr
    strict = cc < rr
    eye = (rr == cc).astype(F32)
    brow = brow_ref[...]
    bcol = bcol_ref[...]
    neg_a_row = -jnp.exp(alrow_ref[...])
    neg_a_col = -jnp.exp(alcol_ref[...])
    gn = gn_ref[...]

    def wy_setup(c):
        rows = slice(c * CHUNK, (c + 1) * CHUNK)
        pre_c = ps_ref[rows, :]
        beta_c = jax.nn.sigmoid(pre_c)
        gc = _dot_sel_left(lower, neg_a_row * _softplus(pre_c + brow))
        gl_r = neg_a_col * _softplus(pst_ref[c] + bcol)
        gr2 = _dot_sel_right(gl_r, upper2)
        chains = []
        for h in range(B_HEADS):
            gcol = gc[:, LANE_A + h:LANE_A + h + 1]
            grow = gr2[LANE_A + h:LANE_A + h + 1, 0:CHUNK]
            beta = beta_c[:, LANE_BETA + h:LANE_BETA + h + 1]
            q = q_ref[rows, h * B_DK:(h + 1) * B_DK]
            k = k_ref[rows, h * B_DK:(h + 1) * B_DK]
            v = v_ref[rows, h * B_DV:(h + 1) * B_DV]
            decay = jnp.where(incl, jnp.exp(jnp.where(incl, gcol - grow, 0.0)), 0.0)
            qk_kk = _dot_nt(jnp.concatenate([q, k], axis=0).astype(BF16), k.astype(BF16))
            pw = jnp.where(strict, -(beta * qk_kk[CHUNK:2 * CHUNK] * decay), 0.0)
            attn_s[c, h] = (qk_kk[0:CHUNK] * decay).astype(BF16)
            e_g = jnp.exp(gcol)
            rhs = jnp.concatenate([v * beta, k * (beta * e_g)], axis=1).astype(BF16)
            g_end = gcol[CHUNK - 1:CHUNK, :]
            wq_s[c, h, CHUNK:2 * CHUNK, :] = (q * e_g).astype(BF16)
            kd_s[c, h] = (k * jnp.exp(g_end - gcol)).astype(BF16)
            ge_s[c, h] = jnp.broadcast_to(jnp.exp(g_end), (1, B_DV))
            chains.append(dict(c=c, h=h, tinv=eye + pw, pw=pw, rhs=rhs))
        return chains

    def wy_solve(chains):
        for ch in chains:
            x_b = ch["pw"].astype(BF16)
            ch["pw"] = _dot(x_b, x_b)
        yield
        for k in range(1, 6):
            for ch in chains:
                pw_b = ch["pw"].astype(BF16)
                if k < 5:
                    r = _dot(jnp.concatenate([ch["tinv"].astype(BF16), pw_b], axis=0), pw_b)
                    ch["tinv"] = ch["tinv"] + r[0:CHUNK]
                    ch["pw"] = r[CHUNK:2 * CHUNK]
                else:
                    ch["tinv"] = ch["tinv"] + _dot(ch["tinv"].astype(BF16), pw_b)
            yield
        for ch in chains:
            c, h = ch["c"], ch["h"]
            sol = _dot(ch["tinv"].astype(BF16), ch["rhs"])
            u_s[c, h] = sol[:, 0:B_DV]
            wq_s[c, h, 0:CHUNK, :] = sol[:, B_DV:2 * B_DV].astype(BF16)
        yield

    def recurrence(chunks):
        for c in chunks:
            rows = slice(c * CHUNK, (c + 1) * CHUNK)
            for h0 in range(0, B_HEADS, HEAD_GROUP):
                pend = []
                for h in range(h0, h0 + HEAD_GROUP):
                    r = _dot(wq_s[c, h], s_s[h].astype(BF16))
                    v_new_b = (u_s[c, h] - r[0:CHUNK]).astype(BF16)
                    s_s[h] = s_s[h] * ge_s[c, h] + _dot_tn(kd_s[c, h], v_new_b)
                    pend.append((h, r[CHUNK:2 * CHUNK], v_new_b))
                yield
                for h, q_s, v_new_b in pend:
                    cols = slice(h * B_DV, (h + 1) * B_DV)
                    o = q_s + _dot(attn_s[c, h], v_new_b)
                    on = o * lax.rsqrt(jnp.mean(o * o, axis=-1, keepdims=True) + RMS_EPS) * gn
                    zz = z_ref[rows, cols]
                    out_ref[rows, cols] = on * (zz * jax.nn.sigmoid(zz))
                yield

    rounds = [range(c0, c0 + GDN_ROUND) for c0 in range(0, GDN_TB // CHUNK, GDN_ROUND)]
    prev = None
    for rnd in rounds:
        chains = [ch for c in rnd for ch in wy_setup(c)]
        _interleave(wy_solve(chains), recurrence(prev) if prev is not None else None)
        prev = rnd
    _interleave(recurrence(prev))


def _gdn(p, pc, ps, pst, brow, bcol, alrow, alcol, gn, batch, seq):
    t = batch * seq
    nb = seq // GDN_TB
    nch = GDN_TB // CHUNK
    tok = lambda b, i: b * nb + i
    const2 = lambda b, i: (0, 0)
    return pl.pallas_call(
        _gdn_kernel,
        grid=(batch, nb),
        in_specs=[pl.BlockSpec((GDN_TB, B_WIDTH), lambda b, i: (tok(b, i), C_COL_QB // B_WIDTH)),
                  pl.BlockSpec((GDN_TB, B_WIDTH), lambda b, i: (tok(b, i), C_COL_KB // B_WIDTH)),
                  pl.BlockSpec((GDN_TB, B_WIDTH), lambda b, i: (tok(b, i), C_COL_VB // B_WIDTH)),
                  pl.BlockSpec((GDN_TB, B_WIDTH), lambda b, i: (tok(b, i), P_COL_ZB // B_WIDTH)),
                  pl.BlockSpec((GDN_TB, GATE_LANES), lambda b, i: (tok(b, i), 0)),
                  pl.BlockSpec((GDN_TB // CHUNK, GATE_ROWS, CHUNK), lambda b, i: (tok(b, i), 0, 0)),
                  pl.BlockSpec((1, GATE_LANES), const2),
                  pl.BlockSpec((GATE_ROWS, 1), const2),
                  pl.BlockSpec((1, GATE_LANES), const2),
                  pl.BlockSpec((GATE_ROWS, 1), const2),
                  pl.BlockSpec((1, B_DV), const2)],
        out_specs=pl.BlockSpec((GDN_TB, B_WIDTH), lambda b, i: (tok(b, i), 0)),
        out_shape=jax.ShapeDtypeStruct((t, B_WIDTH), F32),
        scratch_shapes=[pltpu.VMEM((B_HEADS, B_DK, B_DV), F32),
                        pltpu.VMEM((nch, B_HEADS, CHUNK, B_DV), F32),
                        pltpu.VMEM((nch, B_HEADS, CHUNK, CHUNK), BF16),
                        pltpu.VMEM((nch, B_HEADS, 2 * CHUNK, B_DK), BF16),
                        pltpu.VMEM((nch, B_HEADS, CHUNK, B_DK), BF16),
                        pltpu.VMEM((nch, B_HEADS, 1, B_DV), F32)],
        compiler_params=pltpu.CompilerParams(dimension_semantics=("arbitrary", "arbitrary"),
                                             vmem_limit_bytes=VMEM_LIMIT),
        name="gdn",
    )(pc, pc, pc, p, ps, pst, brow, bcol, alrow, alcol, gn)


MERGE_TM = 512
FFN_TM = 512
FFN_SUB = 256


def _layernorm(r, g, b):
    mu = jnp.mean(r, axis=-1, keepdims=True)
    d = r - mu
    var = jnp.mean(d * d, axis=-1, keepdims=True)
    return d * lax.rsqrt(var + LN_EPS) * g + b


def _merge_kernel(ha_ref, hb_ref, ga_ref, gb_ref, x_ref, wa_ref, wb_ref, wo_ref, g_ref, b_ref, out_ref):
    ya = _dot(ha_ref[...].astype(BF16), wa_ref[...])
    yb = _dot(hb_ref[...].astype(BF16), wb_ref[...])
    y = jax.nn.sigmoid(ga_ref[...]) * ya + jax.nn.sigmoid(gb_ref[...]) * yb
    mix = _dot(y.astype(BF16), wo_ref[...])
    out_ref[...] = _layernorm(DEEPNORM_ALPHA * x_ref[...] + mix, g_ref[...], b_ref[...])


def _merge(ha, hb, p, x, wa, wb, wo, g, b):
    t = x.shape[0]
    row = lambda i: (i, 0)
    const = lambda i: (0, 0)
    wspec = pl.BlockSpec((D_MODEL, D_MODEL), const, pipeline_mode=pl.Buffered(1))
    vspec = pl.BlockSpec((1, D_MODEL), const)
    return pl.pallas_call(
        _merge_kernel,
        grid=(t // MERGE_TM,),
        in_specs=[pl.BlockSpec((MERGE_TM, A_V), row),
                  pl.BlockSpec((MERGE_TM, B_WIDTH), row),
                  pl.BlockSpec((MERGE_TM, D_MODEL), lambda i: (i, P_COL_GA // D_MODEL)),
                  pl.BlockSpec((MERGE_TM, D_MODEL), lambda i: (i, P_COL_GB // D_MODEL)),
                  pl.BlockSpec((MERGE_TM, D_MODEL), row),
                  wspec, wspec, wspec, vspec, vspec],
        out_specs=pl.BlockSpec((MERGE_TM, D_MODEL), row),
        out_shape=jax.ShapeDtypeStruct((t, D_MODEL), F32),
        compiler_params=pltpu.CompilerParams(dimension_semantics=("arbitrary",),
                                             vmem_limit_bytes=VMEM_LIMIT),
        name="merge_ln",
    )(ha, hb, p, p, x, wa, wb, wo, g, b)


def _ffn_kernel(x_ref, wu_ref, wd_ref, g_ref, b_ref, out_ref):
    x = x_ref[...]
    xb = x.astype(BF16)
    acc = None
    for j in range(D_FF // FFN_SUB):
        gt = _dot(xb, wu_ref[:, j * FFN_SUB:(j + 1) * FFN_SUB])
        up = _dot(xb, wu_ref[:, D_FF + j * FFN_SUB:D_FF + (j + 1) * FFN_SUB])
        hid = (gt * jax.nn.sigmoid(gt) * up).astype(BF16)
        part = _dot(hid, wd_ref[j * FFN_SUB:(j + 1) * FFN_SUB, :])
        acc = part if acc is None else acc + part
    out_ref[...] = _layernorm(DEEPNORM_ALPHA * x + acc, g_ref[...], b_ref[...])


def _ffn(x, wu, wd, g, b):
    t = x.shape[0]
    row = lambda i: (i, 0)
    const = lambda i: (0, 0)
    return pl.pallas_call(
        _ffn_kernel,
        grid=(t // FFN_TM,),
        in_specs=[pl.BlockSpec((FFN_TM, D_MODEL), row),
                  pl.BlockSpec((D_MODEL, 2 * D_FF), const, pipeline_mode=pl.Buffered(1)),
                  pl.BlockSpec((D_FF, D_MODEL), const, pipeline_mode=pl.Buffered(1)),
                  pl.BlockSpec((1, D_MODEL), const),
                  pl.BlockSpec((1, D_MODEL), const)],
        out_specs=pl.BlockSpec((FFN_TM, D_MODEL), row),
        out_shape=jax.ShapeDtypeStruct((t, D_MODEL), F32),
        compiler_params=pltpu.CompilerParams(dimension_semantics=("arbitrary",),
                                             vmem_limit_bytes=VMEM_LIMIT),
        name="ffn_ln",
    )(x, wu, wd, g, b)


def _gate_vec(parts, width):
    out = jnp.zeros((DEPTH, width), F32)
    for off, val in parts:
        out = out.at[:, off:off + val.shape[1]].set(val.astype(F32))
    return out


def kernel(x, w_in, b_igate, b_fgate, g_mlstm_norm, conv_w, a_log, dt_bias, g_gdn_norm, w_branch_a, w_branch_b, w_out, ln1_g, ln1_b, w_ffn_up, w_ffn_down, ln2_g, ln2_b):
    batch, seq, _ = x.shape
    t = batch * seq

    o_va = 2 * A_QK
    o_oa = o_va + A_V
    o_if = o_oa + A_V
    o_qb = o_if + 2 * A_HEADS
    o_kb, o_vb, o_zb = o_qb + B_WIDTH, o_qb + 2 * B_WIDTH, o_qb + 3 * B_WIDTH
    o_beta = o_zb + B_WIDTH
    o_gate = o_beta + 2 * B_HEADS
    w_big = jnp.concatenate([w_in[:, :, o_qb:o_kb], w_in[:, :, 0:o_oa],
                             w_in[:, :, o_kb:o_vb], w_in[:, :, o_oa:o_if], w_in[:, :, o_zb:o_beta],
                             w_in[:, :, o_vb:o_zb], w_in[:, :, o_gate:]], axis=-1).astype(BF16)
    w_gates = jnp.concatenate([w_in[:, :, o_if:o_qb], w_in[:, :, o_beta:o_gate]], axis=-1)
    conv_blocks = jnp.swapaxes(conv_w.astype(F32).reshape(DEPTH, CONV_K, 3, CONV_COLS), 1, 2)
    n_g = w_gates.shape[-1]
    w_small = jnp.pad(w_gates, ((0, 0), (0, 0), (0, GATE_LANES - n_g))).astype(BF16)
    w_small_t = jnp.pad(jnp.swapaxes(w_gates, 1, 2), ((0, 0), (0, GATE_ROWS - n_g), (0, 0))).astype(BF16)

    bias_row = _gate_vec([(LANE_I, b_igate), (LANE_F, b_fgate), (LANE_A, dt_bias)], GATE_LANES)
    bias_col = _gate_vec([(LANE_I, b_igate), (LANE_F, b_fgate), (LANE_A, dt_bias)], GATE_ROWS)
    alog_row = _gate_vec([(LANE_A, a_log)], GATE_LANES)
    alog_col = _gate_vec([(LANE_A, a_log)], GATE_ROWS)

    wa = w_branch_a.astype(BF16)
    wb = w_branch_b.astype(BF16)
    wo = w_out.astype(BF16)
    wu = w_ffn_up.astype(BF16)
    wd = w_ffn_down.astype(BF16)

    xt = x.reshape(t, D_MODEL)
    for l in range(DEPTH):
        p, pc = _proj(xt, w_big[l], conv_blocks[l], seq)
        ps, pst = _gate_proj(xt, w_small[l], w_small_t[l])
        brow = bias_row[l][None, :]
        bcol = bias_col[l][:, None]
        ha = _mlstm(p, ps, pst, brow, bcol, g_mlstm_norm[l][None, :].astype(F32), batch, seq)
        hb = _gdn(p, pc, ps, pst, brow, bcol, alog_row[l][None, :], alog_col[l][:, None],
                  g_gdn_norm[l][None, :].astype(F32), batch, seq)
        x1 = _merge(ha, hb, p, xt, wa[l], wb[l], wo[l], ln1_g[l][None, :], ln1_b[l][None, :])
        xt = _ffn(x1, wu[l], wd[l], ln2_g[l][None, :], ln2_b[l][None, :])
    return xt.reshape(batch, seq, D_MODEL)
```

```python
import functools

import jax
import jax.numpy as jnp
from jax import lax
from jax.experimental import pallas as pl
from jax.experimental.pallas import tpu as pltpu

F32 = jnp.float32
BF16 = jnp.bfloat16

D_MODEL = 1024
DEPTH = 4
A_HEADS = 4
A_DK = 128
A_DV = 256
A_QK = A_HEADS * A_DK
A_V = A_HEADS * A_DV
GATE_SOFT_CAP = 15.0
B_HEADS = 8
B_DK = 128
B_DV = 128
B_WIDTH = B_HEADS * B_DK
CONV_K = 4
CHUNK = 64
D_FF = 2816
DEEPNORM_ALPHA = (2.0 * DEPTH) ** 0.25
LN_EPS = 1e-5
RMS_EPS = 1e-6

W_WIDTH = 9216
P_WIDTH = 6144
P_COL_QA, P_COL_KA, P_COL_VA, P_COL_OA, P_COL_ZB, P_COL_GA, P_COL_GB = 0, 512, 1024, 2048, 3072, 4096, 5120
C_WIDTH = 3072
C_COL_QB, C_COL_KB, C_COL_VB = 0, 1024, 2048
GATE_LANES = 128
GATE_ROWS = 32
LANE_I, LANE_F, LANE_BETA, LANE_A = 0, 4, 8, 16
HEAD_GROUP = 4

VMEM_LIMIT = 56 * 1024 * 1024


def _dot(a, b):
    return jnp.dot(a, b, preferred_element_type=F32)


def _dot_nt(a, b):
    return lax.dot_general(a, b, (((1,), (1,)), ((), ())), preferred_element_type=F32)


def _dot_tn(a, b):
    return lax.dot_general(a, b, (((0,), (0,)), ((), ())), preferred_element_type=F32)


def _split3(x):
    x1 = x.astype(BF16)
    r1 = x - x1.astype(F32)
    x2 = r1.astype(BF16)
    x3 = (r1 - x2.astype(F32)).astype(BF16)
    return x1, x2, x3


def _dot_sel_right(x, m):
    x1, x2, x3 = _split3(x)
    return _dot(x1, m) + (_dot(x2, m) + _dot(x3, m))


def _dot_sel_left(m, x):
    x1, x2, x3 = _split3(x)
    return _dot(m, x1) + (_dot(m, x2) + _dot(m, x3))


def _soft_cap(x):
    return GATE_SOFT_CAP * jnp.tanh(x / GATE_SOFT_CAP)


def _log_sigmoid(x):
    return jnp.minimum(x, 0.0) - jnp.log1p(jnp.exp(-jnp.abs(x)))


def _softplus(x):
    return jnp.maximum(x, 0.0) + jnp.log1p(jnp.exp(-jnp.abs(x)))


def _chunk_constants():
    r = lax.broadcasted_iota(jnp.int32, (CHUNK, CHUNK), 0)
    c = lax.broadcasted_iota(jnp.int32, (CHUNK, CHUNK), 1)
    lower = (c <= r).astype(BF16)
    r2 = lax.broadcasted_iota(jnp.int32, (CHUNK, 2 * CHUNK), 0)
    c2 = lax.broadcasted_iota(jnp.int32, (CHUNK, 2 * CHUNK), 1) & (CHUNK - 1)
    upper2 = (r2 <= c2).astype(BF16)
    eye2 = (r2 == c2).astype(BF16)
    return lower, upper2, eye2


PROJ_TM = 512
PROJ_TN = 3072
PROJ_SUB = 512
PROJ_PIECE = 256
CONV_ROWS = 256
CONV_TURN = 2
CONV_COLS = B_WIDTH
CONV_PAD = 8


def _interleave(*streams):
    live = [s if isinstance(s, tuple) else (s, 1) for s in streams if s is not None]
    while live:
        for entry in list(live):
            try:
                for _ in range(entry[1]):
                    next(entry[0])
            except StopIteration:
                live.remove(entry)


def _proj_kernel(x_ref, w_ref, cw_ref, o_ref, c_ref, pad_s, hist_s, *, tiles_per_seq):
    j = pl.program_id(0)
    i = pl.program_id(1)
    xb = x_ref[...].astype(BF16)

    @pl.when(i % tiles_per_seq == 0)
    def _():
        hist_s[j] = jnp.zeros((CONV_PAD, CONV_COLS), F32)

    norm_on = jnp.where(j < 2, 1.0, 0.0).astype(F32)
    q_scale = jnp.where(j == 0, B_DK ** -0.5, 1.0).astype(F32)

    def dot_stream(pieces):
        for pc in pieces:
            cols = slice(pc * PROJ_PIECE, (pc + 1) * PROJ_PIECE)
            y = _dot(xb, w_ref[:, cols])
            if pc * PROJ_PIECE < CONV_COLS:
                half, lanes = divmod(pc * PROJ_PIECE, PROJ_SUB)
                lanes = slice(lanes, lanes + PROJ_PIECE)
                pad_s[half, 0:CONV_PAD, lanes] = hist_s[j, :, cols]
                pad_s[half, CONV_PAD:CONV_PAD + PROJ_TM, lanes] = y
                hist_s[j, :, cols] = y[PROJ_TM - CONV_PAD:PROJ_TM, :]
            else:
                o_ref[:, pc * PROJ_PIECE - CONV_COLS:(pc + 1) * PROJ_PIECE - CONV_COLS] = y
            yield

    def conv_stream(half):
        for g in range(PROJ_SUB // B_DK):
            lanes = slice(g * B_DK, (g + 1) * B_DK)
            cols = slice(half * PROJ_SUB + g * B_DK, half * PROJ_SUB + (g + 1) * B_DK)
            for r in range(0, PROJ_TM, CONV_ROWS):
                acc = None
                for kk in range(CONV_K):
                    r0 = CONV_PAD - (CONV_K - 1) + kk + r
                    term = pad_s[half, r0:r0 + CONV_ROWS, lanes] * cw_ref[0, kk:kk + 1, cols]
                    acc = term if acc is None else acc + term
                a = acc * jax.nn.sigmoid(acc)
                rs = lax.rsqrt(jnp.sum(a * a, axis=-1, keepdims=True) + RMS_EPS) * q_scale
                c_ref[r:r + CONV_ROWS, cols] = a * (rs * norm_on + (1.0 - norm_on))
                yield

    per_half = PROJ_SUB // PROJ_PIECE
    n_conv = CONV_COLS // PROJ_PIECE
    n_all = PROJ_TN // PROJ_PIECE
    _interleave(dot_stream(range(0, per_half)))
    _interleave((conv_stream(0), CONV_TURN), dot_stream(range(per_half, n_conv + 2)))
    _interleave((conv_stream(1), CONV_TURN), dot_stream(range(n_conv + 2, n_conv + 6)))
    _interleave(dot_stream(range(n_conv + 6, n_all)))


def _proj(x, w, cw, seq):
    t = x.shape[0]
    n_blk = W_WIDTH // PROJ_TN
    return pl.pallas_call(
        functools.partial(_proj_kernel, tiles_per_seq=seq // PROJ_TM),
        grid=(n_blk, t // PROJ_TM),
        in_specs=[pl.BlockSpec((PROJ_TM, D_MODEL), lambda j, i: (i, 0)),
                  pl.BlockSpec((D_MODEL, PROJ_TN), lambda j, i: (0, j)),
                  pl.BlockSpec((1, CONV_K, CONV_COLS), lambda j, i: (j, 0, 0))],
        out_specs=[pl.BlockSpec((PROJ_TM, PROJ_TN - CONV_COLS), lambda j, i: (i, j)),
                   pl.BlockSpec((PROJ_TM, CONV_COLS), lambda j, i: (i, j))],
        out_shape=[jax.ShapeDtypeStruct((t, P_WIDTH), F32),
                   jax.ShapeDtypeStruct((t, C_WIDTH), F32)],
        scratch_shapes=[pltpu.VMEM((CONV_COLS // PROJ_SUB, CONV_PAD + PROJ_TM, PROJ_SUB), F32),
                        pltpu.VMEM((n_blk, CONV_PAD, CONV_COLS), F32)],
        compiler_params=pltpu.CompilerParams(dimension_semantics=("arbitrary", "arbitrary"),
                                             vmem_limit_bytes=VMEM_LIMIT),
        name="in_proj",
    )(x, w, cw)


def _gate_proj_kernel(x_ref, ws_ref, wst_ref, ps_ref, pst_ref):
    xb = x_ref[...].astype(BF16)
    ps_ref[...] = _dot(xb, ws_ref[...])
    for c in range(PROJ_TM // CHUNK):
        pst_ref[c] = _dot_nt(wst_ref[...], xb[c * CHUNK:(c + 1) * CHUNK])


def _gate_proj(x, ws, wst):
    t = x.shape[0]
    return pl.pallas_call(
        _gate_proj_kernel,
        grid=(t // PROJ_TM,),
        in_specs=[pl.BlockSpec((PROJ_TM, D_MODEL), lambda i: (i, 0)),
                  pl.BlockSpec((D_MODEL, GATE_LANES), lambda i: (0, 0)),
                  pl.BlockSpec((GATE_ROWS, D_MODEL), lambda i: (0, 0))],
        out_specs=[pl.BlockSpec((PROJ_TM, GATE_LANES), lambda i: (i, 0)),
                   pl.BlockSpec((PROJ_TM // CHUNK, GATE_ROWS, CHUNK), lambda i: (i, 0, 0))],
        out_shape=[jax.ShapeDtypeStruct((t, GATE_LANES), F32),
                   jax.ShapeDtypeStruct((t // CHUNK, GATE_ROWS, CHUNK), F32)],
        compiler_params=pltpu.CompilerParams(dimension_semantics=("arbitrary",),
                                             vmem_limit_bytes=VMEM_LIMIT),
        name="gate_proj",
    )(x, ws, wst)


MLSTM_TB = 512


def _mlstm_kernel(q_ref, k_ref, v_ref, o_ref, ps_ref, pst_ref, brow_ref, bcol_ref, gn_ref,
                  out_ref, c_s, n_s, m_s, nloc_s, rsum_s, mloc_s, gcol_s, kv_s, ksum_s, mdec_s, gend_s):
    @pl.when(pl.program_id(1) == 0)
    def _():
        c_s[...] = jnp.zeros_like(c_s)
        n_s[...] = jnp.zeros_like(n_s)
        m_s[...] = jnp.zeros_like(m_s)

    lower, upper2, eye2 = _chunk_constants()
    incl = (lax.broadcasted_iota(jnp.int32, (CHUNK, CHUNK), 1)
            <= lax.broadcasted_iota(jnp.int32, (CHUNK, CHUNK), 0))
    brow = brow_ref[...]
    bcol = bcol_ref[...]
    scale = A_DK ** -0.5

    n_chunks = MLSTM_TB // CHUNK

    def local(chunks):
        for c in chunks:
            rows = slice(c * CHUNK, (c + 1) * CHUNK)
            cap_c = _soft_cap(ps_ref[rows, :] + brow)
            gc = _dot_sel_left(lower, _log_sigmoid(cap_c))
            cap_r = _soft_cap(pst_ref[c] + bcol)
            gr2 = _dot_sel_right(_log_sigmoid(cap_r), upper2)
            li2 = _dot_sel_right(cap_r, eye2)
            for h in range(A_HEADS):
                gcol = gc[:, LANE_F + h:LANE_F + h + 1]
                row = (gr2[LANE_F + h:LANE_F + h + 1] - li2[LANE_I + h:LANE_I + h + 1])[:, 0:CHUNK]
                q = q_ref[rows, h * A_DK:(h + 1) * A_DK] * scale
                k = k_ref[rows, h * A_DK:(h + 1) * A_DK]
                v = v_ref[rows, h * A_DV:(h + 1) * A_DV].astype(BF16)
                dmat = jnp.where(incl, gcol - row, -jnp.inf)
                m_loc = jnp.max(dmat, axis=-1, keepdims=True)
                s = _dot_nt(q.astype(BF16), k.astype(BF16)) * jnp.exp(dmat - m_loc)
                nloc_s[c, h] = _dot(s.astype(BF16), v)
                rsum_s[c, h] = jnp.sum(s, axis=-1, keepdims=True)
                mloc_s[c, h] = m_loc
                gcol_s[c, h] = gcol
                g_end = gcol[CHUNK - 1:CHUNK, :]
                dec = g_end - gcol + cap_c[:, LANE_I + h:LANE_I + h + 1]
                m_dec = jnp.max(dec, axis=0, keepdims=True)
                kw = jnp.exp(dec - m_dec) * k
                kv_s[c, h] = _dot_tn(kw.astype(BF16), v)
                ksum_s[c, h] = jnp.sum(kw, axis=0, keepdims=True)
                mdec_s[c, h] = m_dec
                gend_s[c, h] = g_end
            yield

    def recur(chunks):
        for c in chunks:
            rows = slice(c * CHUNK, (c + 1) * CHUNK)
            for h in range(A_HEADS):
                cols = slice(h * A_DV, (h + 1) * A_DV)
                mst = m_s[h]
                q = q_ref[rows, h * A_DK:(h + 1) * A_DK] * scale
                inter = gcol_s[c, h] + mst
                m_loc = mloc_s[c, h]
                m_t = jnp.maximum(inter, m_loc)
                q_c = _dot(q.astype(BF16), c_s[h].astype(BF16))
                q_n = jnp.sum(q * n_s[h], axis=-1, keepdims=True)
                g_end = gend_s[c, h]
                m_new = jnp.maximum(g_end + mst, mdec_s[c, h])
                carry_scale = jnp.exp(g_end + mst - m_new)
                local_scale = jnp.exp(mdec_s[c, h] - m_new)
                c_s[h] = carry_scale * c_s[h] + local_scale * kv_s[c, h]
                n_s[h] = carry_scale * n_s[h] + local_scale * ksum_s[c, h]
                m_s[h] = m_new
                a = jnp.exp(m_loc - m_t)
                sc = jnp.exp(inter - m_t)
                num = a * nloc_s[c, h] + sc * q_c
                den = a * rsum_s[c, h] + sc * q_n
                hout = num * (1.0 / jnp.maximum(jnp.abs(den), jnp.exp(-m_t)))
                hn = hout * lax.rsqrt(jnp.mean(hout * hout, axis=-1, keepdims=True) + RMS_EPS)
                out_ref[rows, cols] = (hn * gn_ref[:, cols] * jax.nn.sigmoid(o_ref[rows, cols])).astype(BF16)
            yield

    _interleave(local(range(0, 1)))
    _interleave(local(range(1, n_chunks)), recur(range(n_chunks)))


def _mlstm(p, ps, pst, brow, bcol, gn, batch, seq):
    t = batch * seq
    nb = seq // MLSTM_TB
    nch = MLSTM_TB // CHUNK
    tok = lambda b, i: b * nb + i
    return pl.pallas_call(
        _mlstm_kernel,
        grid=(batch, nb),
        in_specs=[pl.BlockSpec((MLSTM_TB, A_QK), lambda b, i: (tok(b, i), P_COL_QA // A_QK)),
                  pl.BlockSpec((MLSTM_TB, A_QK), lambda b, i: (tok(b, i), P_COL_KA // A_QK)),
                  pl.BlockSpec((MLSTM_TB, A_V), lambda b, i: (tok(b, i), P_COL_VA // A_V)),
                  pl.BlockSpec((MLSTM_TB, A_V), lambda b, i: (tok(b, i), P_COL_OA // A_V)),
                  pl.BlockSpec((MLSTM_TB, GATE_LANES), lambda b, i: (tok(b, i), 0)),
                  pl.BlockSpec((MLSTM_TB // CHUNK, GATE_ROWS, CHUNK), lambda b, i: (tok(b, i), 0, 0)),
                  pl.BlockSpec((1, GATE_LANES), lambda b, i: (0, 0)),
                  pl.BlockSpec((GATE_ROWS, 1), lambda b, i: (0, 0)),
                  pl.BlockSpec((1, A_V), lambda b, i: (0, 0))],
        out_specs=pl.BlockSpec((MLSTM_TB, A_V), lambda b, i: (tok(b, i), 0)),
        out_shape=jax.ShapeDtypeStruct((t, A_V), BF16),
        scratch_shapes=[pltpu.VMEM((A_HEADS, A_DK, A_DV), F32),
                        pltpu.VMEM((A_HEADS, 1, A_DK), F32),
                        pltpu.VMEM((A_HEADS, 1, 1), F32),
                        pltpu.VMEM((nch, A_HEADS, CHUNK, A_DV), F32),
                        pltpu.VMEM((nch, A_HEADS, CHUNK, 1), F32),
                        pltpu.VMEM((nch, A_HEADS, CHUNK, 1), F32),
                        pltpu.VMEM((nch, A_HEADS, CHUNK, 1), F32),
                        pltpu.VMEM((nch, A_HEADS, A_DK, A_DV), F32),
                        pltpu.VMEM((nch, A_HEADS, 1, A_DK), F32),
                        pltpu.VMEM((nch, A_HEADS, 1, 1), F32),
                        pltpu.VMEM((nch, A_HEADS, 1, 1), F32)],
        compiler_params=pltpu.CompilerParams(dimension_semantics=("arbitrary", "arbitrary"),
                                             vmem_limit_bytes=VMEM_LIMIT),
        name="mlstm",
    )(p, p, p, p, ps, pst, brow, bcol, gn)


GDN_TB = 512
GDN_ROUND = 2
GDN_SETUP_TURN = 2


def _gdn_kernel(q_ref, k_ref, v_ref, z_ref, ps_ref, pst_ref, brow_ref, bcol_ref, alrow_ref, alcol_ref, gn_ref,
                out_ref, s_s, u_s, attn_s, wq_s, kd_s, ge_s):
    @pl.when(pl.program_id(1) == 0)
    def _():
        s_s[...] = jnp.zeros_like(s_s)

    lower, upper2, _ = _chunk_constants()
    rr = lax.broadcasted_iota(jnp.int32, (CHUNK, CHUNK), 0)
    cc = lax.broadcasted_iota(jnp.int32, (CHUNK, CHUNK), 1)
    incl = cc <= rr
    strict = cc < rr
    eye = (rr == cc).astype(F32)
    brow = brow_ref[...]
    bcol = bcol_ref[...]
    neg_a_row = -jnp.exp(alrow_ref[...])
    neg_a_col = -jnp.exp(alcol_ref[...])
    gn = gn_ref[...]

    def wy_setup(c, chains):
        rows = slice(c * CHUNK, (c + 1) * CHUNK)
        pre_c = ps_ref[rows, :]
        beta_c = jax.nn.sigmoid(pre_c)
        gc = _dot_sel_left(lower, neg_a_row * _softplus(pre_c + brow))
        gl_r = neg_a_col * _softplus(pst_ref[c] + bcol)
        gr2 = _dot_sel_right(gl_r, upper2)
        for h in range(B_HEADS):
            gcol = gc[:, LANE_A + h:LANE_A + h + 1]
            grow = gr2[LANE_A + h:LANE_A + h + 1, 0:CHUNK]
            beta = beta_c[:, LANE_BETA + h:LANE_BETA + h + 1]
            q = q_ref[rows, h * B_DK:(h + 1) * B_DK]
            k = k_ref[rows, h * B_DK:(h + 1) * B_DK]
            v = v_ref[rows, h * B_DV:(h + 1) * B_DV]
            decay = jnp.where(incl, jnp.exp(jnp.where(incl, gcol - grow, 0.0)), 0.0)
            qk_kk = _dot_nt(jnp.concatenate([q, k], axis=0).astype(BF16), k.astype(BF16))
            pw = jnp.where(strict, -(beta * qk_kk[CHUNK:2 * CHUNK] * decay), 0.0)
            attn_s[c, h] = (qk_kk[0:CHUNK] * decay).astype(BF16)
            e_g = jnp.exp(gcol)
            rhs = jnp.concatenate([v * beta, k * (beta * e_g)], axis=1).astype(BF16)
            g_end = gcol[CHUNK - 1:CHUNK, :]
            wq_s[c, h, CHUNK:2 * CHUNK, :] = (q * e_g).astype(BF16)
            kd_s[c, h] = (k * jnp.exp(g_end - gcol)).astype(BF16)
            ge_s[c, h] = jnp.broadcast_to(jnp.exp(g_end), (1, B_DV))
            chains.append(dict(c=c, h=h, tinv=eye + pw, pw=pw, rhs=rhs))
            yield

    def wy_solve(chains):
        for ch in chains:
            x_b = ch["pw"].astype(BF16)
            ch["pw"] = _dot(x_b, x_b)
        yield
        for k in range(1, 6):
            for ch in chains:
                pw_b = ch["pw"].astype(BF16)
                if k < 5:
                    r = _dot(jnp.concatenate([ch["tinv"].astype(BF16), pw_b], axis=0), pw_b)
                    ch["tinv"] = ch["tinv"] + r[0:CHUNK]
                    ch["pw"] = r[CHUNK:2 * CHUNK]
                else:
                    ch["tinv"] = ch["tinv"] + _dot(ch["tinv"].astype(BF16), pw_b)
            yield
        for ch in chains:
            c, h = ch["c"], ch["h"]
            sol = _dot(ch["tinv"].astype(BF16), ch["rhs"])
            u_s[c, h] = sol[:, 0:B_DV]
            wq_s[c, h, 0:CHUNK, :] = sol[:, B_DV:2 * B_DV].astype(BF16)
        yield

    def recurrence(chunks):
        for c in chunks:
            rows = slice(c * CHUNK, (c + 1) * CHUNK)
            for h0 in range(0, B_HEADS, HEAD_GROUP):
                pend = []
                for h in range(h0, h0 + HEAD_GROUP):
                    r = _dot(wq_s[c, h], s_s[h].astype(BF16))
                    v_new_b = (u_s[c, h] - r[0:CHUNK]).astype(BF16)
                    s_s[h] = s_s[h] * ge_s[c, h] + _dot_tn(kd_s[c, h], v_new_b)
                    pend.append((h, r[CHUNK:2 * CHUNK], v_new_b))
                yield
                for h, q_s, v_new_b in pend:
                    cols = slice(h * B_DV, (h + 1) * B_DV)
                    o = q_s + _dot(attn_s[c, h], v_new_b)
                    on = o * lax.rsqrt(jnp.mean(o * o, axis=-1, keepdims=True) + RMS_EPS) * gn
                    zz = z_ref[rows, cols]
                    out_ref[rows, cols] = (on * (zz * jax.nn.sigmoid(zz))).astype(BF16)
                yield

    def setup_round(rnd, chains):
        for c in rnd:
            yield from wy_setup(c, chains)

    rounds = [range(c0, c0 + GDN_ROUND) for c0 in range(0, GDN_TB // CHUNK, GDN_ROUND)]
    pending = []
    _interleave(setup_round(rounds[0], pending))
    prev = None
    for r, rnd in enumerate(rounds):
        chains, pending = pending, []
        nxt = (setup_round(rounds[r + 1], pending), GDN_SETUP_TURN) if r + 1 < len(rounds) else None
        _interleave(wy_solve(chains), nxt, recurrence(prev) if prev is not None else None)
        prev = rnd
    _interleave(recurrence(prev))


def _gdn(p, pc, ps, pst, brow, bcol, alrow, alcol, gn, batch, seq):
    t = batch * seq
    nb = seq // GDN_TB
    nch = GDN_TB // CHUNK
    tok = lambda b, i: b * nb + i
    const2 = lambda b, i: (0, 0)
    return pl.pallas_call(
        _gdn_kernel,
        grid=(batch, nb),
        in_specs=[pl.BlockSpec((GDN_TB, B_WIDTH), lambda b, i: (tok(b, i), C_COL_QB // B_WIDTH)),
                  pl.BlockSpec((GDN_TB, B_WIDTH), lambda b, i: (tok(b, i), C_COL_KB // B_WIDTH)),
                  pl.BlockSpec((GDN_TB, B_WIDTH), lambda b, i: (tok(b, i), C_COL_VB // B_WIDTH)),
                  pl.BlockSpec((GDN_TB, B_WIDTH), lambda b, i: (tok(b, i), P_COL_ZB // B_WIDTH)),
                  pl.BlockSpec((GDN_TB, GATE_LANES), lambda b, i: (tok(b, i), 0)),
                  pl.BlockSpec((GDN_TB // CHUNK, GATE_ROWS, CHUNK), lambda b, i: (tok(b, i), 0, 0)),
                  pl.BlockSpec((1, GATE_LANES), const2),
                  pl.BlockSpec((GATE_ROWS, 1), const2),
                  pl.BlockSpec((1, GATE_LANES), const2),
                  pl.BlockSpec((GATE_ROWS, 1), const2),
                  pl.BlockSpec((1, B_DV), const2)],
        out_specs=pl.BlockSpec((GDN_TB, B_WIDTH), lambda b, i: (tok(b, i), 0)),
        out_shape=jax.ShapeDtypeStruct((t, B_WIDTH), BF16),
        scratch_shapes=[pltpu.VMEM((B_HEADS, B_DK, B_DV), F32),
                        pltpu.VMEM((nch, B_HEADS, CHUNK, B_DV), F32),
                        pltpu.VMEM((nch, B_HEADS, CHUNK, CHUNK), BF16),
                        pltpu.VMEM((nch, B_HEADS, 2 * CHUNK, B_DK), BF16),
                        pltpu.VMEM((nch, B_HEADS, CHUNK, B_DK), BF16),
                        pltpu.VMEM((nch, B_HEADS, 1, B_DV), F32)],
        compiler_params=pltpu.CompilerParams(dimension_semantics=("arbitrary", "arbitrary"),
                                             vmem_limit_bytes=VMEM_LIMIT),
        name="gdn",
    )(pc, pc, pc, p, ps, pst, brow, bcol, alrow, alcol, gn)


MERGE_TM = 1024
FFN_TM = 512
FFN_SUB = 256


def _layernorm(r, g, b):
    mu = jnp.mean(r, axis=-1, keepdims=True)
    d = r - mu
    var = jnp.mean(d * d, axis=-1, keepdims=True)
    return d * lax.rsqrt(var + LN_EPS) * g + b


def _merge_kernel(ha_ref, hb_ref, ga_ref, gb_ref, x_ref, wa_ref, wb_ref, wo_ref, g_ref, b_ref, out_ref):
    ya = _dot(ha_ref[...], wa_ref[...])
    yb = _dot(hb_ref[...], wb_ref[...])
    y = jax.nn.sigmoid(ga_ref[...]) * ya + jax.nn.sigmoid(gb_ref[...]) * yb
    mix = _dot(y.astype(BF16), wo_ref[...])
    out_ref[...] = _layernorm(DEEPNORM_ALPHA * x_ref[...] + mix, g_ref[...], b_ref[...])


def _merge(ha, hb, p, x, wa, wb, wo, g, b):
    t = x.shape[0]
    row = lambda i: (i, 0)
    const = lambda i: (0, 0)
    wspec = pl.BlockSpec((D_MODEL, D_MODEL), const, pipeline_mode=pl.Buffered(1))
    vspec = pl.BlockSpec((1, D_MODEL), const)
    return pl.pallas_call(
        _merge_kernel,
        grid=(t // MERGE_TM,),
        in_specs=[pl.BlockSpec((MERGE_TM, A_V), row),
                  pl.BlockSpec((MERGE_TM, B_WIDTH), row),
                  pl.BlockSpec((MERGE_TM, D_MODEL), lambda i: (i, P_COL_GA // D_MODEL)),
                  pl.BlockSpec((MERGE_TM, D_MODEL), lambda i: (i, P_COL_GB // D_MODEL)),
                  pl.BlockSpec((MERGE_TM, D_MODEL), row),
                  wspec, wspec, wspec, vspec, vspec],
        out_specs=pl.BlockSpec((MERGE_TM, D_MODEL), row),
        out_shape=jax.ShapeDtypeStruct((t, D_MODEL), F32),
        compiler_params=pltpu.CompilerParams(dimension_semantics=("arbitrary",),
                                             vmem_limit_bytes=VMEM_LIMIT),
        name="merge_ln",
    )(ha, hb, p, p, x, wa, wb, wo, g, b)


def _ffn_kernel(x_ref, wu_ref, wd_ref, g_ref, b_ref, out_ref):
    x = x_ref[...]
    xb = x.astype(BF16)
    acc = None
    for j in range(D_FF // FFN_SUB):
        gt = _dot(xb, wu_ref[:, j * FFN_SUB:(j + 1) * FFN_SUB])
        up = _dot(xb, wu_ref[:, D_FF + j * FFN_SUB:D_FF + (j + 1) * FFN_SUB])
        hid = (gt * jax.nn.sigmoid(gt) * up).astype(BF16)
        part = _dot(hid, wd_ref[j * FFN_SUB:(j + 1) * FFN_SUB, :])
        acc = part if acc is None else acc + part
    out_ref[...] = _layernorm(DEEPNORM_ALPHA * x + acc, g_ref[...], b_ref[...])


def _ffn(x, wu, wd, g, b):
    t = x.shape[0]
    row = lambda i: (i, 0)
    const = lambda i: (0, 0)
    return pl.pallas_call(
        _ffn_kernel,
        grid=(t // FFN_TM,),
        in_specs=[pl.BlockSpec((FFN_TM, D_MODEL), row),
                  pl.BlockSpec((D_MODEL, 2 * D_FF), const, pipeline_mode=pl.Buffered(1)),
                  pl.BlockSpec((D_FF, D_MODEL), const, pipeline_mode=pl.Buffered(1)),
                  pl.BlockSpec((1, D_MODEL), const),
                  pl.BlockSpec((1, D_MODEL), const)],
        out_specs=pl.BlockSpec((FFN_TM, D_MODEL), row),
        out_shape=jax.ShapeDtypeStruct((t, D_MODEL), F32),
        compiler_params=pltpu.CompilerParams(dimension_semantics=("arbitrary",),
                                             vmem_limit_bytes=VMEM_LIMIT),
        name="ffn_ln",
    )(x, wu, wd, g, b)


def _gate_vec(parts, width):
    out = jnp.zeros((DEPTH, width), F32)
    for off, val in parts:
        out = out.at[:, off:off + val.shape[1]].set(val.astype(F32))
    return out


def kernel(x, w_in, b_igate, b_fgate, g_mlstm_norm, conv_w, a_log, dt_bias, g_gdn_norm, w_branch_a, w_branch_b, w_out, ln1_g, ln1_b, w_ffn_up, w_ffn_down, ln2_g, ln2_b):
    batch, seq, _ = x.shape
    t = batch * seq

    o_va = 2 * A_QK
    o_oa = o_va + A_V
    o_if = o_oa + A_V
    o_qb = o_if + 2 * A_HEADS
    o_kb, o_vb, o_zb = o_qb + B_WIDTH, o_qb + 2 * B_WIDTH, o_qb + 3 * B_WIDTH
    o_beta = o_zb + B_WIDTH
    o_gate = o_beta + 2 * B_HEADS
    w_big = jnp.concatenate([w_in[:, :, o_qb:o_kb], w_in[:, :, 0:o_oa],
                             w_in[:, :, o_kb:o_vb], w_in[:, :, o_oa:o_if], w_in[:, :, o_zb:o_beta],
                             w_in[:, :, o_vb:o_zb], w_in[:, :, o_gate:]], axis=-1).astype(BF16)
    w_gates = jnp.concatenate([w_in[:, :, o_if:o_qb], w_in[:, :, o_beta:o_gate]], axis=-1)
    conv_blocks = jnp.swapaxes(conv_w.astype(F32).reshape(DEPTH, CONV_K, 3, CONV_COLS), 1, 2)
    n_g = w_gates.shape[-1]
    w_small = jnp.pad(w_gates, ((0, 0), (0, 0), (0, GATE_LANES - n_g))).astype(BF16)
    w_small_t = jnp.pad(jnp.swapaxes(w_gates, 1, 2), ((0, 0), (0, GATE_ROWS - n_g), (0, 0))).astype(BF16)

    bias_row = _gate_vec([(LANE_I, b_igate), (LANE_F, b_fgate), (LANE_A, dt_bias)], GATE_LANES)
    bias_col = _gate_vec([(LANE_I, b_igate), (LANE_F, b_fgate), (LANE_A, dt_bias)], GATE_ROWS)
    alog_row = _gate_vec([(LANE_A, a_log)], GATE_LANES)
    alog_col = _gate_vec([(LANE_A, a_log)], GATE_ROWS)

    wa = w_branch_a.astype(BF16)
    wb = w_branch_b.astype(BF16)
    wo = w_out.astype(BF16)
    wu = w_ffn_up.astype(BF16)
    wd = w_ffn_down.astype(BF16)

    xt = x.reshape(t, D_MODEL)
    for l in range(DEPTH):
        p, pc = _proj(xt, w_big[l], conv_blocks[l], seq)
        ps, pst = _gate_proj(xt, w_small[l], w_small_t[l])
        brow = bias_row[l][None, :]
        bcol = bias_col[l][:, None]
        ha = _mlstm(p, ps, pst, brow, bcol, g_mlstm_norm[l][None, :].astype(F32), batch, seq)
        hb = _gdn(p, pc, ps, pst, brow, bcol, alog_row[l][None, :], alog_col[l][:, None],
                  g_gdn_norm[l][None, :].astype(F32), batch, seq)
        x1 = _merge(ha, hb, p, xt, wa[l], wb[l], wo[l], ln1_g[l][None, :], ln1_b[l][None, :])
        xt = _ffn(x1, wu[l], wd[l], ln2_g[l][None, :], ln2_b[l][None, :])
    return xt.reshape(batch, seq, D_MODEL)
```

```python
import functools

import jax
import jax.numpy as jnp
from jax import lax
from jax.experimental import pallas as pl
from jax.experimental.pallas import tpu as pltpu

F32 = jnp.float32
BF16 = jnp.bfloat16

D_MODEL = 1024
DEPTH = 4
A_HEADS = 4
A_DK = 128
A_DV = 256
A_QK = A_HEADS * A_DK
A_V = A_HEADS * A_DV
GATE_SOFT_CAP = 15.0
B_HEADS = 8
B_DK = 128
B_DV = 128
B_WIDTH = B_HEADS * B_DK
CONV_K = 4
CHUNK = 64
D_FF = 2816
DEEPNORM_ALPHA = (2.0 * DEPTH) ** 0.25
LN_EPS = 1e-5
RMS_EPS = 1e-6

W_WIDTH = 9216
P_WIDTH = 6144
P_COL_QA, P_COL_KA, P_COL_VA, P_COL_OA, P_COL_ZB, P_COL_GA, P_COL_GB = 0, 512, 1024, 2048, 3072, 4096, 5120
C_WIDTH = 3072
C_COL_QB, C_COL_KB, C_COL_VB = 0, 1024, 2048
GATE_LANES = 128
GATE_ROWS = 32
LANE_I, LANE_F, LANE_BETA, LANE_A = 0, 4, 8, 16
HEAD_GROUP = 8

VMEM_LIMIT = 56 * 1024 * 1024


def _dot(a, b):
    return jnp.dot(a, b, preferred_element_type=F32)


def _dot_nt(a, b):
    return lax.dot_general(a, b, (((1,), (1,)), ((), ())), preferred_element_type=F32)


def _dot_tn(a, b):
    return lax.dot_general(a, b, (((0,), (0,)), ((), ())), preferred_element_type=F32)


def _split3(x):
    x1 = x.astype(BF16)
    r1 = x - x1.astype(F32)
    x2 = r1.astype(BF16)
    x3 = (r1 - x2.astype(F32)).astype(BF16)
    return x1, x2, x3


def _dot_sel_right(x, m):
    x1, x2, x3 = _split3(x)
    return _dot(x1, m) + (_dot(x2, m) + _dot(x3, m))


def _dot_sel_left(m, x):
    x1, x2, x3 = _split3(x)
    return _dot(m, x1) + (_dot(m, x2) + _dot(m, x3))


def _soft_cap(x):
    return GATE_SOFT_CAP * jnp.tanh(x / GATE_SOFT_CAP)


def _log_sigmoid(x):
    return jnp.minimum(x, 0.0) - jnp.log1p(jnp.exp(-jnp.abs(x)))


def _softplus(x):
    return jnp.maximum(x, 0.0) + jnp.log1p(jnp.exp(-jnp.abs(x)))


def _chunk_constants():
    r = lax.broadcasted_iota(jnp.int32, (CHUNK, CHUNK), 0)
    c = lax.broadcasted_iota(jnp.int32, (CHUNK, CHUNK), 1)
    lower = (c <= r).astype(BF16)
    r2 = lax.broadcasted_iota(jnp.int32, (CHUNK, 2 * CHUNK), 0)
    c2 = lax.broadcasted_iota(jnp.int32, (CHUNK, 2 * CHUNK), 1) & (CHUNK - 1)
    upper2 = (r2 <= c2).astype(BF16)
    eye2 = (r2 == c2).astype(BF16)
    return lower, upper2, eye2


PROJ_TM = 512
PROJ_TN = 3072
PROJ_SUB = 512
PROJ_PIECE = 256
CONV_ROWS = 256
CONV_TURN = 2
CONV_COLS = B_WIDTH
CONV_PAD = 8


def _interleave(*streams):
    live = [s if isinstance(s, tuple) else (s, 1) for s in streams if s is not None]
    while live:
        for entry in list(live):
            try:
                for _ in range(entry[1]):
                    next(entry[0])
            except StopIteration:
                live.remove(entry)


def _proj_kernel(x_ref, w_ref, cw_ref, o_ref, c_ref, pad_s, hist_s, *, tiles_per_seq):
    j = pl.program_id(0)
    i = pl.program_id(1)
    xb = x_ref[...].astype(BF16)

    @pl.when(i % tiles_per_seq == 0)
    def _():
        hist_s[j] = jnp.zeros((CONV_PAD, CONV_COLS), F32)

    norm_on = jnp.where(j < 2, 1.0, 0.0).astype(F32)
    q_scale = jnp.where(j == 0, B_DK ** -0.5, 1.0).astype(F32)

    def dot_stream(pieces):
        for pc in pieces:
            cols = slice(pc * PROJ_PIECE, (pc + 1) * PROJ_PIECE)
            y = _dot(xb, w_ref[:, cols])
            if pc * PROJ_PIECE < CONV_COLS:
                half, lanes = divmod(pc * PROJ_PIECE, PROJ_SUB)
                lanes = slice(lanes, lanes + PROJ_PIECE)
                pad_s[half, 0:CONV_PAD, lanes] = hist_s[j, :, cols]
                pad_s[half, CONV_PAD:CONV_PAD + PROJ_TM, lanes] = y
                hist_s[j, :, cols] = y[PROJ_TM - CONV_PAD:PROJ_TM, :]
            else:
                o_ref[:, pc * PROJ_PIECE - CONV_COLS:(pc + 1) * PROJ_PIECE - CONV_COLS] = y
            yield

    def conv_stream(half):
        for g in range(PROJ_SUB // B_DK):
            lanes = slice(g * B_DK, (g + 1) * B_DK)
            cols = slice(half * PROJ_SUB + g * B_DK, half * PROJ_SUB + (g + 1) * B_DK)
            for r in range(0, PROJ_TM, CONV_ROWS):
                acc = None
                for kk in range(CONV_K):
                    r0 = CONV_PAD - (CONV_K - 1) + kk + r
                    term = pad_s[half, r0:r0 + CONV_ROWS, lanes] * cw_ref[0, kk:kk + 1, cols]
                    acc = term if acc is None else acc + term
                a = acc * jax.nn.sigmoid(acc)
                rs = lax.rsqrt(jnp.sum(a * a, axis=-1, keepdims=True) + RMS_EPS) * q_scale
                c_ref[r:r + CONV_ROWS, cols] = a * (rs * norm_on + (1.0 - norm_on))
                yield

    per_half = PROJ_SUB // PROJ_PIECE
    n_conv = CONV_COLS // PROJ_PIECE
    n_all = PROJ_TN // PROJ_PIECE
    _interleave(dot_stream(range(0, per_half)))
    _interleave((conv_stream(0), CONV_TURN), dot_stream(range(per_half, n_conv + 2)))
    _interleave((conv_stream(1), CONV_TURN), dot_stream(range(n_conv + 2, n_conv + 6)))
    _interleave(dot_stream(range(n_conv + 6, n_all)))


def _proj(x, w, cw, seq):
    t = x.shape[0]
    n_blk = W_WIDTH // PROJ_TN
    return pl.pallas_call(
        functools.partial(_proj_kernel, tiles_per_seq=seq // PROJ_TM),
        grid=(n_blk, t // PROJ_TM),
        in_specs=[pl.BlockSpec((PROJ_TM, D_MODEL), lambda j, i: (i, 0)),
                  pl.BlockSpec((D_MODEL, PROJ_TN), lambda j, i: (0, j)),
                  pl.BlockSpec((1, CONV_K, CONV_COLS), lambda j, i: (j, 0, 0))],
        out_specs=[pl.BlockSpec((PROJ_TM, PROJ_TN - CONV_COLS), lambda j, i: (i, j)),
                   pl.BlockSpec((PROJ_TM, CONV_COLS), lambda j, i: (i, j))],
        out_shape=[jax.ShapeDtypeStruct((t, P_WIDTH), F32),
                   jax.ShapeDtypeStruct((t, C_WIDTH), F32)],
        scratch_shapes=[pltpu.VMEM((CONV_COLS // PROJ_SUB, CONV_PAD + PROJ_TM, PROJ_SUB), F32),
                        pltpu.VMEM((n_blk, CONV_PAD, CONV_COLS), F32)],
        compiler_params=pltpu.CompilerParams(dimension_semantics=("arbitrary", "arbitrary"),
                                             vmem_limit_bytes=VMEM_LIMIT),
        name="in_proj",
    )(x, w, cw)


def _gate_proj_kernel(x_ref, ws_ref, wst_ref, ps_ref, pst_ref):
    xb = x_ref[...].astype(BF16)
    ps_ref[...] = _dot(xb, ws_ref[...])
    for c in range(PROJ_TM // CHUNK):
        pst_ref[c] = _dot_nt(wst_ref[...], xb[c * CHUNK:(c + 1) * CHUNK])


def _gate_proj(x, ws, wst):
    t = x.shape[0]
    return pl.pallas_call(
        _gate_proj_kernel,
        grid=(t // PROJ_TM,),
        in_specs=[pl.BlockSpec((PROJ_TM, D_MODEL), lambda i: (i, 0)),
                  pl.BlockSpec((D_MODEL, GATE_LANES), lambda i: (0, 0)),
                  pl.BlockSpec((GATE_ROWS, D_MODEL), lambda i: (0, 0))],
        out_specs=[pl.BlockSpec((PROJ_TM, GATE_LANES), lambda i: (i, 0)),
                   pl.BlockSpec((PROJ_TM // CHUNK, GATE_ROWS, CHUNK), lambda i: (i, 0, 0))],
        out_shape=[jax.ShapeDtypeStruct((t, GATE_LANES), F32),
                   jax.ShapeDtypeStruct((t // CHUNK, GATE_ROWS, CHUNK), F32)],
        compiler_params=pltpu.CompilerParams(dimension_semantics=("arbitrary",),
                                             vmem_limit_bytes=VMEM_LIMIT),
        name="gate_proj",
    )(x, ws, wst)


MLSTM_TB = 512


def _mlstm_kernel(q_ref, k_ref, v_ref, o_ref, ps_ref, pst_ref, brow_ref, bcol_ref, gn_ref,
                  out_ref, c_s, n_s, m_s, nloc_s, rsum_s, mloc_s, gcol_s, kv_s, ksum_s, mdec_s, gend_s):
    @pl.when(pl.program_id(1) == 0)
    def _():
        c_s[...] = jnp.zeros_like(c_s)
        n_s[...] = jnp.zeros_like(n_s)
        m_s[...] = jnp.zeros_like(m_s)

    lower, upper2, eye2 = _chunk_constants()
    incl = (lax.broadcasted_iota(jnp.int32, (CHUNK, CHUNK), 1)
            <= lax.broadcasted_iota(jnp.int32, (CHUNK, CHUNK), 0))
    brow = brow_ref[...]
    bcol = bcol_ref[...]
    scale = A_DK ** -0.5

    n_chunks = MLSTM_TB // CHUNK

    def local(chunks):
        for c in chunks:
            rows = slice(c * CHUNK, (c + 1) * CHUNK)
            cap_c = _soft_cap(ps_ref[rows, :] + brow)
            gc = _dot_sel_left(lower, _log_sigmoid(cap_c))
            cap_r = _soft_cap(pst_ref[c] + bcol)
            gr2 = _dot_sel_right(_log_sigmoid(cap_r), upper2)
            li2 = _dot_sel_right(cap_r, eye2)
            for h in range(A_HEADS):
                gcol = gc[:, LANE_F + h:LANE_F + h + 1]
                row = (gr2[LANE_F + h:LANE_F + h + 1] - li2[LANE_I + h:LANE_I + h + 1])[:, 0:CHUNK]
                q = q_ref[rows, h * A_DK:(h + 1) * A_DK] * scale
                k = k_ref[rows, h * A_DK:(h + 1) * A_DK]
                v = v_ref[rows, h * A_DV:(h + 1) * A_DV].astype(BF16)
                dmat = jnp.where(incl, gcol - row, -jnp.inf)
                m_loc = jnp.max(dmat, axis=-1, keepdims=True)
                s = _dot_nt(q.astype(BF16), k.astype(BF16)) * jnp.exp(dmat - m_loc)
                nloc_s[c, h] = _dot(s.astype(BF16), v)
                rsum_s[c, h] = jnp.sum(s, axis=-1, keepdims=True)
                mloc_s[c, h] = m_loc
                gcol_s[c, h] = gcol
                g_end = gcol[CHUNK - 1:CHUNK, :]
                dec = g_end - gcol + cap_c[:, LANE_I + h:LANE_I + h + 1]
                m_dec = jnp.max(dec, axis=0, keepdims=True)
                kw = jnp.exp(dec - m_dec) * k
                kv_s[c, h] = _dot_tn(kw.astype(BF16), v)
                ksum_s[c, h] = jnp.sum(kw, axis=0, keepdims=True)
                mdec_s[c, h] = m_dec
                gend_s[c, h] = g_end
            yield

    def recur(chunks):
        for c in chunks:
            rows = slice(c * CHUNK, (c + 1) * CHUNK)
            for h in range(A_HEADS):
                cols = slice(h * A_DV, (h + 1) * A_DV)
                mst = m_s[h]
                q = q_ref[rows, h * A_DK:(h + 1) * A_DK] * scale
                inter = gcol_s[c, h] + mst
                m_loc = mloc_s[c, h]
                m_t = jnp.maximum(inter, m_loc)
                q_c = _dot(q.astype(BF16), c_s[h].astype(BF16))
                q_n = jnp.sum(q * n_s[h], axis=-1, keepdims=True)
                g_end = gend_s[c, h]
                m_new = jnp.maximum(g_end + mst, mdec_s[c, h])
                carry_scale = jnp.exp(g_end + mst - m_new)
                local_scale = jnp.exp(mdec_s[c, h] - m_new)
                c_s[h] = carry_scale * c_s[h] + local_scale * kv_s[c, h]
                n_s[h] = carry_scale * n_s[h] + local_scale * ksum_s[c, h]
                m_s[h] = m_new
                a = jnp.exp(m_loc - m_t)
                sc = jnp.exp(inter - m_t)
                num = a * nloc_s[c, h] + sc * q_c
                den = a * rsum_s[c, h] + sc * q_n
                hout = num * (1.0 / jnp.maximum(jnp.abs(den), jnp.exp(-m_t)))
                hn = hout * lax.rsqrt(jnp.mean(hout * hout, axis=-1, keepdims=True) + RMS_EPS)
                out_ref[rows, cols] = (hn * gn_ref[:, cols] * jax.nn.sigmoid(o_ref[rows, cols])).astype(BF16)
            yield

    _interleave(local(range(0, 1)))
    _interleave(local(range(1, n_chunks)), recur(range(n_chunks)))


def _mlstm(p, ps, pst, brow, bcol, gn, batch, seq):
    t = batch * seq
    nb = seq // MLSTM_TB
    nch = MLSTM_TB // CHUNK
    tok = lambda b, i: b * nb + i
    return pl.pallas_call(
        _mlstm_kernel,
        grid=(batch, nb),
        in_specs=[pl.BlockSpec((MLSTM_TB, A_QK), lambda b, i: (tok(b, i), P_COL_QA // A_QK)),
                  pl.BlockSpec((MLSTM_TB, A_QK), lambda b, i: (tok(b, i), P_COL_KA // A_QK)),
                  pl.BlockSpec((MLSTM_TB, A_V), lambda b, i: (tok(b, i), P_COL_VA // A_V)),
                  pl.BlockSpec((MLSTM_TB, A_V), lambda b, i: (tok(b, i), P_COL_OA // A_V)),
                  pl.BlockSpec((MLSTM_TB, GATE_LANES), lambda b, i: (tok(b, i), 0)),
                  pl.BlockSpec((MLSTM_TB // CHUNK, GATE_ROWS, CHUNK), lambda b, i: (tok(b, i), 0, 0)),
                  pl.BlockSpec((1, GATE_LANES), lambda b, i: (0, 0)),
                  pl.BlockSpec((GATE_ROWS, 1), lambda b, i: (0, 0)),
                  pl.BlockSpec((1, A_V), lambda b, i: (0, 0))],
        out_specs=pl.BlockSpec((MLSTM_TB, A_V), lambda b, i: (tok(b, i), 0)),
        out_shape=jax.ShapeDtypeStruct((t, A_V), BF16),
        scratch_shapes=[pltpu.VMEM((A_HEADS, A_DK, A_DV), F32),
                        pltpu.VMEM((A_HEADS, 1, A_DK), F32),
                        pltpu.VMEM((A_HEADS, 1, 1), F32),
                        pltpu.VMEM((nch, A_HEADS, CHUNK, A_DV), F32),
                        pltpu.VMEM((nch, A_HEADS, CHUNK, 1), F32),
                        pltpu.VMEM((nch, A_HEADS, CHUNK, 1), F32),
                        pltpu.VMEM((nch, A_HEADS, CHUNK, 1), F32),
                        pltpu.VMEM((nch, A_HEADS, A_DK, A_DV), F32),
                        pltpu.VMEM((nch, A_HEADS, 1, A_DK), F32),
                        pltpu.VMEM((nch, A_HEADS, 1, 1), F32),
                        pltpu.VMEM((nch, A_HEADS, 1, 1), F32)],
        compiler_params=pltpu.CompilerParams(dimension_semantics=("arbitrary", "arbitrary"),
                                             vmem_limit_bytes=VMEM_LIMIT),
        name="mlstm",
    )(p, p, p, p, ps, pst, brow, bcol, gn)


GDN_TB = 512
GDN_ROUND = 2
GDN_SETUP_TURN = 2


def _gdn_kernel(q_ref, k_ref, v_ref, z_ref, ps_ref, pst_ref, brow_ref, bcol_ref, alrow_ref, alcol_ref, gn_ref,
                out_ref, s_s, u_s, attn_s, wq_s, kd_s, ge_s):
    @pl.when(pl.program_id(1) == 0)
    def _():
        s_s[...] = jnp.zeros_like(s_s)

    lower, upper2, _ = _chunk_constants()
    rr = lax.broadcasted_iota(jnp.int32, (CHUNK, CHUNK), 0)
    cc = lax.broadcasted_iota(jnp.int32, (CHUNK, CHUNK), 1)
    incl = cc <= rr
    strict = cc < rr
    eye = (rr == cc).astype(F32)
    brow = brow_ref[...]
    bcol = bcol_ref[...]
    neg_a_row = -jnp.exp(alrow_ref[...])
    neg_a_col = -jnp.exp(alcol_ref[...])
    gn = gn_ref[...]

    def wy_setup(c, chains):
        rows = slice(c * CHUNK, (c + 1) * CHUNK)
        pre_c = ps_ref[rows, :]
        beta_c = jax.nn.sigmoid(pre_c)
        gc = _dot_sel_left(lower, neg_a_row * _softplus(pre_c + brow))
        gl_r = neg_a_col * _softplus(pst_ref[c] + bcol)
        gr2 = _dot_sel_right(gl_r, upper2)
        for h in range(B_HEADS):
            gcol = gc[:, LANE_A + h:LANE_A + h + 1]
            grow = gr2[LANE_A + h:LANE_A + h + 1, 0:CHUNK]
            beta = beta_c[:, LANE_BETA + h:LANE_BETA + h + 1]
            q = q_ref[rows, h * B_DK:(h + 1) * B_DK]
            k = k_ref[rows, h * B_DK:(h + 1) * B_DK]
            v = v_ref[rows, h * B_DV:(h + 1) * B_DV]
            decay = jnp.where(incl, jnp.exp(jnp.where(incl, gcol - grow, 0.0)), 0.0)
            qk_kk = _dot_nt(jnp.concatenate([q, k], axis=0).astype(BF16), k.astype(BF16))
            pw = jnp.where(strict, -(beta * qk_kk[CHUNK:2 * CHUNK] * decay), 0.0)
            attn_s[c, h] = (qk_kk[0:CHUNK] * decay).astype(BF16)
            e_g = jnp.exp(gcol)
            rhs = jnp.concatenate([v * beta, k * (beta * e_g)], axis=1).astype(BF16)
            g_end = gcol[CHUNK - 1:CHUNK, :]
            wq_s[c, h, CHUNK:2 * CHUNK, :] = (q * e_g).astype(BF16)
            kd_s[c, h] = (k * jnp.exp(g_end - gcol)).astype(BF16)
            ge_s[c, h] = jnp.broadcast_to(jnp.exp(g_end), (1, B_DV))
            chains.append(dict(c=c, h=h, tinv=eye + pw, pw=pw, rhs=rhs))
            yield

    def wy_solve(chains):
        for ch in chains:
            x_b = ch["pw"].astype(BF16)
            ch["pw"] = _dot(x_b, x_b)
        yield
        for k in range(1, 6):
            for i, ch in enumerate(chains):
                pw_b = ch["pw"].astype(BF16)
                if k < 5:
                    r = _dot(jnp.concatenate([ch["tinv"].astype(BF16), pw_b], axis=0), pw_b)
                    ch["tinv"] = ch["tinv"] + r[0:CHUNK]
                    ch["pw"] = r[CHUNK:2 * CHUNK]
                else:
                    ch["tinv"] = ch["tinv"] + _dot(ch["tinv"].astype(BF16), pw_b)
                if 2 * (i + 1) == len(chains):
                    yield
            yield
        for ch in chains:
            c, h = ch["c"], ch["h"]
            sol = _dot(ch["tinv"].astype(BF16), ch["rhs"])
            u_s[c, h] = sol[:, 0:B_DV]
            wq_s[c, h, 0:CHUNK, :] = sol[:, B_DV:2 * B_DV].astype(BF16)
        yield

    def recurrence(chunks):
        for c in chunks:
            rows = slice(c * CHUNK, (c + 1) * CHUNK)
            for h0 in range(0, B_HEADS, HEAD_GROUP):
                pend = []
                for h in range(h0, h0 + HEAD_GROUP):
                    r = _dot(wq_s[c, h], s_s[h].astype(BF16))
                    v_new_b = (u_s[c, h] - r[0:CHUNK]).astype(BF16)
                    s_s[h] = s_s[h] * ge_s[c, h] + _dot_tn(kd_s[c, h], v_new_b)
                    pend.append((h, r[CHUNK:2 * CHUNK], v_new_b))
                yield
                for h, q_s, v_new_b in pend:
                    cols = slice(h * B_DV, (h + 1) * B_DV)
                    o = q_s + _dot(attn_s[c, h], v_new_b)
                    on = o * lax.rsqrt(jnp.mean(o * o, axis=-1, keepdims=True) + RMS_EPS) * gn
                    zz = z_ref[rows, cols]
                    out_ref[rows, cols] = (on * (zz * jax.nn.sigmoid(zz))).astype(BF16)
                yield

    def setup_round(rnd, chains):
        for c in rnd:
            yield from wy_setup(c, chains)

    rounds = [range(c0, c0 + GDN_ROUND) for c0 in range(0, GDN_TB // CHUNK, GDN_ROUND)]
    pending = []
    _interleave(setup_round(rounds[0], pending))
    prev = None
    for r, rnd in enumerate(rounds):
        chains, pending = pending, []
        nxt = (setup_round(rounds[r + 1], pending), GDN_SETUP_TURN) if r + 1 < len(rounds) else None
        _interleave(wy_solve(chains), nxt, recurrence(prev) if prev is not None else None)
        prev = rnd
    _interleave(recurrence(prev))


def _gdn(p, pc, ps, pst, brow, bcol, alrow, alcol, gn, batch, seq):
    t = batch * seq
    nb = seq // GDN_TB
    nch = GDN_TB // CHUNK
    tok = lambda b, i: b * nb + i
    const2 = lambda b, i: (0, 0)
    return pl.pallas_call(
        _gdn_kernel,
        grid=(batch, nb),
        in_specs=[pl.BlockSpec((GDN_TB, B_WIDTH), lambda b, i: (tok(b, i), C_COL_QB // B_WIDTH)),
                  pl.BlockSpec((GDN_TB, B_WIDTH), lambda b, i: (tok(b, i), C_COL_KB // B_WIDTH)),
                  pl.BlockSpec((GDN_TB, B_WIDTH), lambda b, i: (tok(b, i), C_COL_VB // B_WIDTH)),
                  pl.BlockSpec((GDN_TB, B_WIDTH), lambda b, i: (tok(b, i), P_COL_ZB // B_WIDTH)),
                  pl.BlockSpec((GDN_TB, GATE_LANES), lambda b, i: (tok(b, i), 0)),
                  pl.BlockSpec((GDN_TB // CHUNK, GATE_ROWS, CHUNK), lambda b, i: (tok(b, i), 0, 0)),
                  pl.BlockSpec((1, GATE_LANES), const2),
                  pl.BlockSpec((GATE_ROWS, 1), const2),
                  pl.BlockSpec((1, GATE_LANES), const2),
                  pl.BlockSpec((GATE_ROWS, 1), const2),
                  pl.BlockSpec((1, B_DV), const2)],
        out_specs=pl.BlockSpec((GDN_TB, B_WIDTH), lambda b, i: (tok(b, i), 0)),
        out_shape=jax.ShapeDtypeStruct((t, B_WIDTH), BF16),
        scratch_shapes=[pltpu.VMEM((B_HEADS, B_DK, B_DV), F32),
                        pltpu.VMEM((nch, B_HEADS, CHUNK, B_DV), F32),
                        pltpu.VMEM((nch, B_HEADS, CHUNK, CHUNK), BF16),
                        pltpu.VMEM((nch, B_HEADS, 2 * CHUNK, B_DK), BF16),
                        pltpu.VMEM((nch, B_HEADS, CHUNK, B_DK), BF16),
                        pltpu.VMEM((nch, B_HEADS, 1, B_DV), F32)],
        compiler_params=pltpu.CompilerParams(dimension_semantics=("arbitrary", "arbitrary"),
                                             vmem_limit_bytes=VMEM_LIMIT),
        name="gdn",
    )(pc, pc, pc, p, ps, pst, brow, bcol, alrow, alcol, gn)


MERGE_TM = 1024
FFN_TM = 512
FFN_SUB = 256


def _layernorm(r, g, b):
    mu = jnp.mean(r, axis=-1, keepdims=True)
    d = r - mu
    var = jnp.mean(d * d, axis=-1, keepdims=True)
    return d * lax.rsqrt(var + LN_EPS) * g + b


def _merge_kernel(ha_ref, hb_ref, ga_ref, gb_ref, x_ref, wa_ref, wb_ref, wo_ref, g_ref, b_ref, out_ref):
    ya = _dot(ha_ref[...], wa_ref[...])
    yb = _dot(hb_ref[...], wb_ref[...])
    y = jax.nn.sigmoid(ga_ref[...]) * ya + jax.nn.sigmoid(gb_ref[...]) * yb
    mix = _dot(y.astype(BF16), wo_ref[...])
    out_ref[...] = _layernorm(DEEPNORM_ALPHA * x_ref[...] + mix, g_ref[...], b_ref[...])


def _merge(ha, hb, p, x, wa, wb, wo, g, b):
    t = x.shape[0]
    row = lambda i: (i, 0)
    const = lambda i: (0, 0)
    wspec = pl.BlockSpec((D_MODEL, D_MODEL), const, pipeline_mode=pl.Buffered(1))
    vspec = pl.BlockSpec((1, D_MODEL), const)
    return pl.pallas_call(
        _merge_kernel,
        grid=(t // MERGE_TM,),
        in_specs=[pl.BlockSpec((MERGE_TM, A_V), row),
                  pl.BlockSpec((MERGE_TM, B_WIDTH), row),
                  pl.BlockSpec((MERGE_TM, D_MODEL), lambda i: (i, P_COL_GA // D_MODEL)),
                  pl.BlockSpec((MERGE_TM, D_MODEL), lambda i: (i, P_COL_GB // D_MODEL)),
                  pl.BlockSpec((MERGE_TM, D_MODEL), row),
                  wspec, wspec, wspec, vspec, vspec],
        out_specs=pl.BlockSpec((MERGE_TM, D_MODEL), row),
        out_shape=jax.ShapeDtypeStruct((t, D_MODEL), F32),
        compiler_params=pltpu.CompilerParams(dimension_semantics=("arbitrary",),
                                             vmem_limit_bytes=VMEM_LIMIT),
        name="merge_ln",
    )(ha, hb, p, p, x, wa, wb, wo, g, b)


def _ffn_kernel(x_ref, wu_ref, wd_ref, g_ref, b_ref, out_ref):
    x = x_ref[...]
    xb = x.astype(BF16)
    acc = None
    for j in range(D_FF // FFN_SUB):
        gt = _dot(xb, wu_ref[:, j * FFN_SUB:(j + 1) * FFN_SUB])
        up = _dot(xb, wu_ref[:, D_FF + j * FFN_SUB:D_FF + (j + 1) * FFN_SUB])
        hid = (gt * jax.nn.sigmoid(gt) * up).astype(BF16)
        part = _dot(hid, wd_ref[j * FFN_SUB:(j + 1) * FFN_SUB, :])
        acc = part if acc is None else acc + part
    out_ref[...] = _layernorm(DEEPNORM_ALPHA * x + acc, g_ref[...], b_ref[...])


def _ffn(x, wu, wd, g, b):
    t = x.shape[0]
    row = lambda i: (i, 0)
    const = lambda i: (0, 0)
    return pl.pallas_call(
        _ffn_kernel,
        grid=(t // FFN_TM,),
        in_specs=[pl.BlockSpec((FFN_TM, D_MODEL), row),
                  pl.BlockSpec((D_MODEL, 2 * D_FF), const, pipeline_mode=pl.Buffered(1)),
                  pl.BlockSpec((D_FF, D_MODEL), const, pipeline_mode=pl.Buffered(1)),
                  pl.BlockSpec((1, D_MODEL), const),
                  pl.BlockSpec((1, D_MODEL), const)],
        out_specs=pl.BlockSpec((FFN_TM, D_MODEL), row),
        out_shape=jax.ShapeDtypeStruct((t, D_MODEL), F32),
        compiler_params=pltpu.CompilerParams(dimension_semantics=("arbitrary",),
                                             vmem_limit_bytes=VMEM_LIMIT),
        name="ffn_ln",
    )(x, wu, wd, g, b)


def _gate_vec(parts, width):
    out = jnp.zeros((DEPTH, width), F32)
    for off, val in parts:
        out = out.at[:, off:off + val.shape[1]].set(val.astype(F32))
    return out


def kernel(x, w_in, b_igate, b_fgate, g_mlstm_norm, conv_w, a_log, dt_bias, g_gdn_norm, w_branch_a, w_branch_b, w_out, ln1_g, ln1_b, w_ffn_up, w_ffn_down, ln2_g, ln2_b):
    batch, seq, _ = x.shape
    t = batch * seq

    o_va = 2 * A_QK
    o_oa = o_va + A_V
    o_if = o_oa + A_V
    o_qb = o_if + 2 * A_HEADS
    o_kb, o_vb, o_zb = o_qb + B_WIDTH, o_qb + 2 * B_WIDTH, o_qb + 3 * B_WIDTH
    o_beta = o_zb + B_WIDTH
    o_gate = o_beta + 2 * B_HEADS
    w_big = jnp.concatenate([w_in[:, :, o_qb:o_kb], w_in[:, :, 0:o_oa],
                             w_in[:, :, o_kb:o_vb], w_in[:, :, o_oa:o_if], w_in[:, :, o_zb:o_beta],
                             w_in[:, :, o_vb:o_zb], w_in[:, :, o_gate:]], axis=-1).astype(BF16)
    w_gates = jnp.concatenate([w_in[:, :, o_if:o_qb], w_in[:, :, o_beta:o_gate]], axis=-1)
    conv_blocks = jnp.swapaxes(conv_w.astype(F32).reshape(DEPTH, CONV_K, 3, CONV_COLS), 1, 2)
    n_g = w_gates.shape[-1]
    w_small = jnp.pad(w_gates, ((0, 0), (0, 0), (0, GATE_LANES - n_g))).astype(BF16)
    w_small_t = jnp.pad(jnp.swapaxes(w_gates, 1, 2), ((0, 0), (0, GATE_ROWS - n_g), (0, 0))).astype(BF16)

    bias_row = _gate_vec([(LANE_I, b_igate), (LANE_F, b_fgate), (LANE_A, dt_bias)], GATE_LANES)
    bias_col = _gate_vec([(LANE_I, b_igate), (LANE_F, b_fgate), (LANE_A, dt_bias)], GATE_ROWS)
    alog_row = _gate_vec([(LANE_A, a_log)], GATE_LANES)
    alog_col = _gate_vec([(LANE_A, a_log)], GATE_ROWS)

    wa = w_branch_a.astype(BF16)
    wb = w_branch_b.astype(BF16)
    wo = w_out.astype(BF16)
    wu = w_ffn_up.astype(BF16)
    wd = w_ffn_down.astype(BF16)

    xt = x.reshape(t, D_MODEL)
    for l in range(DEPTH):
        p, pc = _proj(xt, w_big[l], conv_blocks[l], seq)
        ps, pst = _gate_proj(xt, w_small[l], w_small_t[l])
        brow = bias_row[l][None, :]
        bcol = bias_col[l][:, None]
        ha = _mlstm(p, ps, pst, brow, bcol, g_mlstm_norm[l][None, :].astype(F32), batch, seq)
        hb = _gdn(p, pc, ps, pst, brow, bcol, alog_row[l][None, :], alog_col[l][:, None],
                  g_gdn_norm[l][None, :].astype(F32), batch, seq)
        x1 = _merge(ha, hb, p, xt, wa[l], wb[l], wo[l], ln1_g[l][None, :], ln1_b[l][None, :])
        xt = _ffn(x1, wu[l], wd[l], ln2_g[l][None, :], ln2_b[l][None, :])
    return xt.reshape(batch, seq, D_MODEL)
```

```python
import functools

import jax
import jax.numpy as jnp
from jax import lax
from jax.experimental import pallas as pl
from jax.experimental.pallas import tpu as pltpu

F32 = jnp.float32
BF16 = jnp.bfloat16

D_MODEL = 1024
DEPTH = 4
A_HEADS = 4
A_DK = 128
A_DV = 256
A_QK = A_HEADS * A_DK
A_V = A_HEADS * A_DV
GATE_SOFT_CAP = 15.0
B_HEADS = 8
B_DK = 128
B_DV = 128
B_WIDTH = B_HEADS * B_DK
CONV_K = 4
CHUNK = 64
D_FF = 2816
DEEPNORM_ALPHA = (2.0 * DEPTH) ** 0.25
LN_EPS = 1e-5
RMS_EPS = 1e-6

W_WIDTH = 9216
P_WIDTH = 6144
P_COL_QA, P_COL_KA, P_COL_VA, P_COL_OA, P_COL_ZB, P_COL_GA, P_COL_GB = 0, 512, 1024, 2048, 3072, 4096, 5120
C_WIDTH = 3072
C_COL_QB, C_COL_KB, C_COL_VB = 0, 1024, 2048
GATE_LANES = 128
GATE_ROWS = 32
LANE_I, LANE_F, LANE_BETA, LANE_A = 0, 4, 8, 16
HEAD_GROUP = 8

VMEM_LIMIT = 56 * 1024 * 1024


def _dot(a, b):
    return jnp.dot(a, b, preferred_element_type=F32)


def _dot_nt(a, b):
    return lax.dot_general(a, b, (((1,), (1,)), ((), ())), preferred_element_type=F32)


def _dot_tn(a, b):
    return lax.dot_general(a, b, (((0,), (0,)), ((), ())), preferred_element_type=F32)


def _split3(x):
    x1 = x.astype(BF16)
    r1 = x - x1.astype(F32)
    x2 = r1.astype(BF16)
    x3 = (r1 - x2.astype(F32)).astype(BF16)
    return x1, x2, x3


def _dot_sel_right(x, m):
    x1, x2, x3 = _split3(x)
    return _dot(x1, m) + (_dot(x2, m) + _dot(x3, m))


def _dot_sel_left(m, x):
    x1, x2, x3 = _split3(x)
    return _dot(m, x1) + (_dot(m, x2) + _dot(m, x3))


def _soft_cap(x):
    return GATE_SOFT_CAP * jnp.tanh(x / GATE_SOFT_CAP)


def _log_sigmoid(x):
    return jnp.minimum(x, 0.0) - jnp.log1p(jnp.exp(-jnp.abs(x)))


def _softplus(x):
    return jnp.maximum(x, 0.0) + jnp.log1p(jnp.exp(-jnp.abs(x)))


def _chunk_constants():
    r = lax.broadcasted_iota(jnp.int32, (CHUNK, CHUNK), 0)
    c = lax.broadcasted_iota(jnp.int32, (CHUNK, CHUNK), 1)
    lower = (c <= r).astype(BF16)
    r2 = lax.broadcasted_iota(jnp.int32, (CHUNK, 2 * CHUNK), 0)
    c2 = lax.broadcasted_iota(jnp.int32, (CHUNK, 2 * CHUNK), 1) & (CHUNK - 1)
    upper2 = (r2 <= c2).astype(BF16)
    eye2 = (r2 == c2).astype(BF16)
    return lower, upper2, eye2


PROJ_TM = 512
PROJ_TN = 3072
PROJ_SUB = 512
PROJ_PIECE = 256
CONV_ROWS = 256
CONV_TURN = 2
CONV_COLS = B_WIDTH
CONV_PAD = 8


def _interleave(*streams):
    live = [s if isinstance(s, tuple) else (s, 1) for s in streams if s is not None]
    while live:
        for entry in list(live):
            try:
                for _ in range(entry[1]):
                    next(entry[0])
            except StopIteration:
                live.remove(entry)


def _proj_kernel(x_ref, w_ref, cw_ref, o_ref, c_ref, pad_s, hist_s, *, tiles_per_seq):
    j = pl.program_id(0)
    i = pl.program_id(1)
    xb = x_ref[...].astype(BF16)

    @pl.when(i % tiles_per_seq == 0)
    def _():
        hist_s[j] = jnp.zeros((CONV_PAD, CONV_COLS), F32)

    norm_on = jnp.where(j < 2, 1.0, 0.0).astype(F32)
    q_scale = jnp.where(j == 0, B_DK ** -0.5, 1.0).astype(F32)

    def dot_stream(pieces):
        for pc in pieces:
            cols = slice(pc * PROJ_PIECE, (pc + 1) * PROJ_PIECE)
            y = _dot(xb, w_ref[:, cols])
            if pc * PROJ_PIECE < CONV_COLS:
                half, lanes = divmod(pc * PROJ_PIECE, PROJ_SUB)
                lanes = slice(lanes, lanes + PROJ_PIECE)
                pad_s[half, 0:CONV_PAD, lanes] = hist_s[j, :, cols]
                pad_s[half, CONV_PAD:CONV_PAD + PROJ_TM, lanes] = y
                hist_s[j, :, cols] = y[PROJ_TM - CONV_PAD:PROJ_TM, :]
            else:
                o_ref[:, pc * PROJ_PIECE - CONV_COLS:(pc + 1) * PROJ_PIECE - CONV_COLS] = y
            yield

    def conv_stream(half):
        for g in range(PROJ_SUB // B_DK):
            lanes = slice(g * B_DK, (g + 1) * B_DK)
            cols = slice(half * PROJ_SUB + g * B_DK, half * PROJ_SUB + (g + 1) * B_DK)
            for r in range(0, PROJ_TM, CONV_ROWS):
                acc = None
                for kk in range(CONV_K):
                    r0 = CONV_PAD - (CONV_K - 1) + kk + r
                    term = pad_s[half, r0:r0 + CONV_ROWS, lanes] * cw_ref[0, kk:kk + 1, cols]
                    acc = term if acc is None else acc + term
                a = acc * jax.nn.sigmoid(acc)
                rs = lax.rsqrt(jnp.sum(a * a, axis=-1, keepdims=True) + RMS_EPS) * q_scale
                c_ref[r:r + CONV_ROWS, cols] = a * (rs * norm_on + (1.0 - norm_on))
                yield

    per_half = PROJ_SUB // PROJ_PIECE
    n_conv = CONV_COLS // PROJ_PIECE
    n_all = PROJ_TN // PROJ_PIECE
    _interleave(dot_stream(range(0, per_half)))
    _interleave((conv_stream(0), CONV_TURN), dot_stream(range(per_half, n_conv + 2)))
    _interleave((conv_stream(1), CONV_TURN), dot_stream(range(n_conv + 2, n_conv + 6)))
    _interleave(dot_stream(range(n_conv + 6, n_all)))


def _proj(x, w, cw, seq):
    t = x.shape[0]
    n_blk = W_WIDTH // PROJ_TN
    return pl.pallas_call(
        functools.partial(_proj_kernel, tiles_per_seq=seq // PROJ_TM),
        grid=(n_blk, t // PROJ_TM),
        in_specs=[pl.BlockSpec((PROJ_TM, D_MODEL), lambda j, i: (i, 0)),
                  pl.BlockSpec((D_MODEL, PROJ_TN), lambda j, i: (0, j)),
                  pl.BlockSpec((1, CONV_K, CONV_COLS), lambda j, i: (j, 0, 0))],
        out_specs=[pl.BlockSpec((PROJ_TM, PROJ_TN - CONV_COLS), lambda j, i: (i, j)),
                   pl.BlockSpec((PROJ_TM, CONV_COLS), lambda j, i: (i, j))],
        out_shape=[jax.ShapeDtypeStruct((t, P_WIDTH), F32),
                   jax.ShapeDtypeStruct((t, C_WIDTH), F32)],
        scratch_shapes=[pltpu.VMEM((CONV_COLS // PROJ_SUB, CONV_PAD + PROJ_TM, PROJ_SUB), F32),
                        pltpu.VMEM((n_blk, CONV_PAD, CONV_COLS), F32)],
        compiler_params=pltpu.CompilerParams(dimension_semantics=("arbitrary", "arbitrary"),
                                             vmem_limit_bytes=VMEM_LIMIT),
        name="in_proj",
    )(x, w, cw)


def _gate_proj_kernel(x_ref, ws_ref, wst_ref, ps_ref, pst_ref):
    xb = x_ref[...].astype(BF16)
    ps_ref[...] = _dot(xb, ws_ref[...])
    for c in range(PROJ_TM // CHUNK):
        pst_ref[c] = _dot_nt(wst_ref[...], xb[c * CHUNK:(c + 1) * CHUNK])


def _gate_proj(x, ws, wst):
    t = x.shape[0]
    return pl.pallas_call(
        _gate_proj_kernel,
        grid=(t // PROJ_TM,),
        in_specs=[pl.BlockSpec((PROJ_TM, D_MODEL), lambda i: (i, 0)),
                  pl.BlockSpec((D_MODEL, GATE_LANES), lambda i: (0, 0)),
                  pl.BlockSpec((GATE_ROWS, D_MODEL), lambda i: (0, 0))],
        out_specs=[pl.BlockSpec((PROJ_TM, GATE_LANES), lambda i: (i, 0)),
                   pl.BlockSpec((PROJ_TM // CHUNK, GATE_ROWS, CHUNK), lambda i: (i, 0, 0))],
        out_shape=[jax.ShapeDtypeStruct((t, GATE_LANES), F32),
                   jax.ShapeDtypeStruct((t // CHUNK, GATE_ROWS, CHUNK), F32)],
        compiler_params=pltpu.CompilerParams(dimension_semantics=("arbitrary",),
                                             vmem_limit_bytes=VMEM_LIMIT),
        name="gate_proj",
    )(x, ws, wst)


MLSTM_TB = 512


def _mlstm_kernel(q_ref, k_ref, v_ref, o_ref, ps_ref, pst_ref, brow_ref, bcol_ref, gn_ref,
                  out_ref, c_s, n_s, m_s, nloc_s, rsum_s, mloc_s, gcol_s, kv_s, ksum_s, mdec_s, gend_s):
    @pl.when(pl.program_id(1) == 0)
    def _():
        c_s[...] = jnp.zeros_like(c_s)
        n_s[...] = jnp.zeros_like(n_s)
        m_s[...] = jnp.zeros_like(m_s)

    lower, upper2, eye2 = _chunk_constants()
    incl = (lax.broadcasted_iota(jnp.int32, (CHUNK, CHUNK), 1)
            <= lax.broadcasted_iota(jnp.int32, (CHUNK, CHUNK), 0))
    brow = brow_ref[...]
    bcol = bcol_ref[...]
    scale = A_DK ** -0.5

    n_chunks = MLSTM_TB // CHUNK

    def local(chunks):
        for c in chunks:
            rows = slice(c * CHUNK, (c + 1) * CHUNK)
            cap_c = _soft_cap(ps_ref[rows, :] + brow)
            gc = _dot_sel_left(lower, _log_sigmoid(cap_c))
            cap_r = _soft_cap(pst_ref[c] + bcol)
            gr2 = _dot_sel_right(_log_sigmoid(cap_r), upper2)
            li2 = _dot_sel_right(cap_r, eye2)
            for h in range(A_HEADS):
                gcol = gc[:, LANE_F + h:LANE_F + h + 1]
                row = (gr2[LANE_F + h:LANE_F + h + 1] - li2[LANE_I + h:LANE_I + h + 1])[:, 0:CHUNK]
                q = q_ref[rows, h * A_DK:(h + 1) * A_DK] * scale
                k = k_ref[rows, h * A_DK:(h + 1) * A_DK]
                v = v_ref[rows, h * A_DV:(h + 1) * A_DV].astype(BF16)
                dmat = jnp.where(incl, gcol - row, -jnp.inf)
                m_loc = jnp.max(dmat, axis=-1, keepdims=True)
                s = _dot_nt(q.astype(BF16), k.astype(BF16)) * jnp.exp(dmat - m_loc)
                nloc_s[c, h] = _dot(s.astype(BF16), v)
                rsum_s[c, h] = jnp.sum(s, axis=-1, keepdims=True)
                mloc_s[c, h] = m_loc
                gcol_s[c, h] = gcol
                g_end = gcol[CHUNK - 1:CHUNK, :]
                dec = g_end - gcol + cap_c[:, LANE_I + h:LANE_I + h + 1]
                m_dec = jnp.max(dec, axis=0, keepdims=True)
                kw = jnp.exp(dec - m_dec) * k
                kv_s[c, h] = _dot_tn(kw.astype(BF16), v)
                ksum_s[c, h] = jnp.sum(kw, axis=0, keepdims=True)
                mdec_s[c, h] = m_dec
                gend_s[c, h] = g_end
            yield

    def recur(chunks):
        for c in chunks:
            rows = slice(c * CHUNK, (c + 1) * CHUNK)
            for h in range(A_HEADS):
                cols = slice(h * A_DV, (h + 1) * A_DV)
                mst = m_s[h]
                q = q_ref[rows, h * A_DK:(h + 1) * A_DK] * scale
                inter = gcol_s[c, h] + mst
                m_loc = mloc_s[c, h]
                m_t = jnp.maximum(inter, m_loc)
                q_c = _dot(q.astype(BF16), c_s[h].astype(BF16))
                q_n = jnp.sum(q * n_s[h], axis=-1, keepdims=True)
                g_end = gend_s[c, h]
                m_new = jnp.maximum(g_end + mst, mdec_s[c, h])
                carry_scale = jnp.exp(g_end + mst - m_new)
                local_scale = jnp.exp(mdec_s[c, h] - m_new)
                c_s[h] = carry_scale * c_s[h] + local_scale * kv_s[c, h]
                n_s[h] = carry_scale * n_s[h] + local_scale * ksum_s[c, h]
                m_s[h] = m_new
                a = jnp.exp(m_loc - m_t)
                sc = jnp.exp(inter - m_t)
                num = a * nloc_s[c, h] + sc * q_c
                den = a * rsum_s[c, h] + sc * q_n
                hout = num * (1.0 / jnp.maximum(jnp.abs(den), jnp.exp(-m_t)))
                hn = hout * lax.rsqrt(jnp.mean(hout * hout, axis=-1, keepdims=True) + RMS_EPS)
                out_ref[rows, cols] = (hn * gn_ref[:, cols] * jax.nn.sigmoid(o_ref[rows, cols])).astype(BF16)
            yield

    _interleave(local(range(0, 1)))
    _interleave(local(range(1, n_chunks)), recur(range(n_chunks)))


def _mlstm(p, ps, pst, brow, bcol, gn, batch, seq):
    t = batch * seq
    nb = seq // MLSTM_TB
    nch = MLSTM_TB // CHUNK
    tok = lambda b, i: b * nb + i
    return pl.pallas_call(
        _mlstm_kernel,
        grid=(batch, nb),
        in_specs=[pl.BlockSpec((MLSTM_TB, A_QK), lambda b, i: (tok(b, i), P_COL_QA // A_QK)),
                  pl.BlockSpec((MLSTM_TB, A_QK), lambda b, i: (tok(b, i), P_COL_KA // A_QK)),
                  pl.BlockSpec((MLSTM_TB, A_V), lambda b, i: (tok(b, i), P_COL_VA // A_V)),
                  pl.BlockSpec((MLSTM_TB, A_V), lambda b, i: (tok(b, i), P_COL_OA // A_V)),
                  pl.BlockSpec((MLSTM_TB, GATE_LANES), lambda b, i: (tok(b, i), 0)),
                  pl.BlockSpec((MLSTM_TB // CHUNK, GATE_ROWS, CHUNK), lambda b, i: (tok(b, i), 0, 0)),
                  pl.BlockSpec((1, GATE_LANES), lambda b, i: (0, 0)),
                  pl.BlockSpec((GATE_ROWS, 1), lambda b, i: (0, 0)),
                  pl.BlockSpec((1, A_V), lambda b, i: (0, 0))],
        out_specs=pl.BlockSpec((MLSTM_TB, A_V), lambda b, i: (tok(b, i), 0)),
        out_shape=jax.ShapeDtypeStruct((t, A_V), BF16),
        scratch_shapes=[pltpu.VMEM((A_HEADS, A_DK, A_DV), F32),
                        pltpu.VMEM((A_HEADS, 1, A_DK), F32),
                        pltpu.VMEM((A_HEADS, 1, 1), F32),
                        pltpu.VMEM((nch, A_HEADS, CHUNK, A_DV), F32),
                        pltpu.VMEM((nch, A_HEADS, CHUNK, 1), F32),
                        pltpu.VMEM((nch, A_HEADS, CHUNK, 1), F32),
                        pltpu.VMEM((nch, A_HEADS, CHUNK, 1), F32),
                        pltpu.VMEM((nch, A_HEADS, A_DK, A_DV), F32),
                        pltpu.VMEM((nch, A_HEADS, 1, A_DK), F32),
                        pltpu.VMEM((nch, A_HEADS, 1, 1), F32),
                        pltpu.VMEM((nch, A_HEADS, 1, 1), F32)],
        compiler_params=pltpu.CompilerParams(dimension_semantics=("arbitrary", "arbitrary"),
                                             vmem_limit_bytes=VMEM_LIMIT),
        name="mlstm",
    )(p, p, p, p, ps, pst, brow, bcol, gn)


GDN_TB = 512
GDN_ROUND = 2
GDN_SETUP_TURN = 2


def _gdn_kernel(q_ref, k_ref, v_ref, z_ref, ps_ref, pst_ref, brow_ref, bcol_ref, alrow_ref, alcol_ref, gn_ref,
                out_ref, s_s, u_s, attn_s, wq_s, kd_s, ge_s):
    @pl.when(pl.program_id(1) == 0)
    def _():
        s_s[...] = jnp.zeros_like(s_s)

    lower, upper2, _ = _chunk_constants()
    rr = lax.broadcasted_iota(jnp.int32, (CHUNK, CHUNK), 0)
    cc = lax.broadcasted_iota(jnp.int32, (CHUNK, CHUNK), 1)
    incl = cc <= rr
    strict = cc < rr
    eye = (rr == cc).astype(F32)
    brow = brow_ref[...]
    bcol = bcol_ref[...]
    neg_a_row = -jnp.exp(alrow_ref[...])
    neg_a_col = -jnp.exp(alcol_ref[...])
    gn = gn_ref[...]

    def wy_setup(c, chains):
        rows = slice(c * CHUNK, (c + 1) * CHUNK)
        pre_c = ps_ref[rows, :]
        beta_c = jax.nn.sigmoid(pre_c)
        gc = _dot_sel_left(lower, neg_a_row * _softplus(pre_c + brow))
        gl_r = neg_a_col * _softplus(pst_ref[c] + bcol)
        gr2 = _dot_sel_right(gl_r, upper2)
        for h in range(B_HEADS):
            gcol = gc[:, LANE_A + h:LANE_A + h + 1]
            grow = gr2[LANE_A + h:LANE_A + h + 1, 0:CHUNK]
            beta = beta_c[:, LANE_BETA + h:LANE_BETA + h + 1]
            q = q_ref[rows, h * B_DK:(h + 1) * B_DK]
            k = k_ref[rows, h * B_DK:(h + 1) * B_DK]
            v = v_ref[rows, h * B_DV:(h + 1) * B_DV]
            decay = jnp.where(incl, jnp.exp(jnp.where(incl, gcol - grow, 0.0)), 0.0)
            qk_kk = _dot_nt(jnp.concatenate([q, k], axis=0).astype(BF16), k.astype(BF16))
            pw = jnp.where(strict, -(beta * qk_kk[CHUNK:2 * CHUNK] * decay), 0.0)
            attn_s[c, h] = (qk_kk[0:CHUNK] * decay).astype(BF16)
            e_g = jnp.exp(gcol)
            rhs = jnp.concatenate([v * beta, k * (beta * e_g)], axis=1).astype(BF16)
            g_end = gcol[CHUNK - 1:CHUNK, :]
            wq_s[c, h, CHUNK:2 * CHUNK, :] = (q * e_g).astype(BF16)
            kd_s[c, h] = (k * jnp.exp(g_end - gcol)).astype(BF16)
            ge_s[c, h] = jnp.broadcast_to(jnp.exp(g_end), (1, B_DV))
            chains.append(dict(c=c, h=h, tinv=eye + pw, pw=pw, rhs=rhs))
            yield

    def wy_solve(chains):
        for ch in chains:
            x_b = ch["pw"].astype(BF16)
            ch["pw"] = _dot(x_b, x_b)
        yield
        for k in range(1, 6):
            for i, ch in enumerate(chains):
                pw_b = ch["pw"].astype(BF16)
                if k < 5:
                    r = _dot(jnp.concatenate([ch["tinv"].astype(BF16), pw_b], axis=0), pw_b)
                    ch["tinv"] = ch["tinv"] + r[0:CHUNK]
                    ch["pw"] = r[CHUNK:2 * CHUNK]
                else:
                    ch["tinv"] = ch["tinv"] + _dot(ch["tinv"].astype(BF16), pw_b)
                if 2 * (i + 1) == len(chains):
                    yield
            yield
        for ch in chains:
            c, h = ch["c"], ch["h"]
            sol = _dot(ch["tinv"].astype(BF16), ch["rhs"])
            u_s[c, h] = sol[:, 0:B_DV]
            wq_s[c, h, 0:CHUNK, :] = sol[:, B_DV:2 * B_DV].astype(BF16)
        yield

    def recurrence(chunks):
        for c in chunks:
            rows = slice(c * CHUNK, (c + 1) * CHUNK)
            for h0 in range(0, B_HEADS, HEAD_GROUP):
                pend = []
                for h in range(h0, h0 + HEAD_GROUP):
                    r = _dot(wq_s[c, h], s_s[h].astype(BF16))
                    v_new_b = (u_s[c, h] - r[0:CHUNK]).astype(BF16)
                    s_s[h] = s_s[h] * ge_s[c, h] + _dot_tn(kd_s[c, h], v_new_b)
                    pend.append((h, r[CHUNK:2 * CHUNK], v_new_b))
                yield
                for h, q_s, v_new_b in pend:
                    cols = slice(h * B_DV, (h + 1) * B_DV)
                    o = q_s + _dot(attn_s[c, h], v_new_b)
                    on = o * lax.rsqrt(jnp.mean(o * o, axis=-1, keepdims=True) + RMS_EPS) * gn
                    zz = z_ref[rows, cols]
                    out_ref[rows, cols] = (on * (zz * jax.nn.sigmoid(zz))).astype(BF16)
                yield

    def setup_round(rnd, chains):
        for c in rnd:
            yield from wy_setup(c, chains)

    rounds = [range(c0, c0 + GDN_ROUND) for c0 in range(0, GDN_TB // CHUNK, GDN_ROUND)]
    pending = []
    _interleave(setup_round(rounds[0], pending))
    prev = None
    for r, rnd in enumerate(rounds):
        chains, pending = pending, []
        nxt = (setup_round(rounds[r + 1], pending), GDN_SETUP_TURN) if r + 1 < len(rounds) else None
        _interleave(wy_solve(chains), nxt, recurrence(prev) if prev is not None else None)
        prev = rnd
    _interleave(recurrence(prev))


def _gdn(p, pc, ps, pst, brow, bcol, alrow, alcol, gn, batch, seq):
    t = batch * seq
    nb = seq // GDN_TB
    nch = GDN_TB // CHUNK
    tok = lambda b, i: b * nb + i
    const2 = lambda b, i: (0, 0)
    return pl.pallas_call(
        _gdn_kernel,
        grid=(batch, nb),
        in_specs=[pl.BlockSpec((GDN_TB, B_WIDTH), lambda b, i: (tok(b, i), C_COL_QB // B_WIDTH)),
                  pl.BlockSpec((GDN_TB, B_WIDTH), lambda b, i: (tok(b, i), C_COL_KB // B_WIDTH)),
                  pl.BlockSpec((GDN_TB, B_WIDTH), lambda b, i: (tok(b, i), C_COL_VB // B_WIDTH)),
                  pl.BlockSpec((GDN_TB, B_WIDTH), lambda b, i: (tok(b, i), P_COL_ZB // B_WIDTH)),
                  pl.BlockSpec((GDN_TB, GATE_LANES), lambda b, i: (tok(b, i), 0)),
                  pl.BlockSpec((GDN_TB // CHUNK, GATE_ROWS, CHUNK), lambda b, i: (tok(b, i), 0, 0)),
                  pl.BlockSpec((1, GATE_LANES), const2),
                  pl.BlockSpec((GATE_ROWS, 1), const2),
                  pl.BlockSpec((1, GATE_LANES), const2),
                  pl.BlockSpec((GATE_ROWS, 1), const2),
                  pl.BlockSpec((1, B_DV), const2)],
        out_specs=pl.BlockSpec((GDN_TB, B_WIDTH), lambda b, i: (tok(b, i), 0)),
        out_shape=jax.ShapeDtypeStruct((t, B_WIDTH), BF16),
        scratch_shapes=[pltpu.VMEM((B_HEADS, B_DK, B_DV), F32),
                        pltpu.VMEM((nch, B_HEADS, CHUNK, B_DV), F32),
                        pltpu.VMEM((nch, B_HEADS, CHUNK, CHUNK), BF16),
                        pltpu.VMEM((nch, B_HEADS, 2 * CHUNK, B_DK), BF16),
                        pltpu.VMEM((nch, B_HEADS, CHUNK, B_DK), BF16),
                        pltpu.VMEM((nch, B_HEADS, 1, B_DV), F32)],
        compiler_params=pltpu.CompilerParams(dimension_semantics=("arbitrary", "arbitrary"),
                                             vmem_limit_bytes=VMEM_LIMIT),
        name="gdn",
    )(pc, pc, pc, p, ps, pst, brow, bcol, alrow, alcol, gn)


MERGE_TM = 1024
FFN_TM = 512
FFN_SUB = 256


def _layernorm(r, g, b):
    mu = jnp.mean(r, axis=-1, keepdims=True)
    d = r - mu
    var = jnp.mean(d * d, axis=-1, keepdims=True)
    return d * lax.rsqrt(var + LN_EPS) * g + b


def _merge_kernel(ha_ref, hb_ref, ga_ref, gb_ref, x_ref, wa_ref, wb_ref, wo_ref, g_ref, b_ref, out_ref):
    ya = _dot(ha_ref[...], wa_ref[...])
    yb = _dot(hb_ref[...], wb_ref[...])
    y = jax.nn.sigmoid(ga_ref[...]) * ya + jax.nn.sigmoid(gb_ref[...]) * yb
    mix = _dot(y.astype(BF16), wo_ref[...])
    out_ref[...] = _layernorm(DEEPNORM_ALPHA * x_ref[...] + mix, g_ref[...], b_ref[...])


def _merge(ha, hb, p, x, wa, wb, wo, g, b):
    t = x.shape[0]
    row = lambda i: (i, 0)
    const = lambda i: (0, 0)
    wspec = pl.BlockSpec((D_MODEL, D_MODEL), const, pipeline_mode=pl.Buffered(1))
    vspec = pl.BlockSpec((1, D_MODEL), const)
    return pl.pallas_call(
        _merge_kernel,
        grid=(t // MERGE_TM,),
        in_specs=[pl.BlockSpec((MERGE_TM, A_V), row),
                  pl.BlockSpec((MERGE_TM, B_WIDTH), row),
                  pl.BlockSpec((MERGE_TM, D_MODEL), lambda i: (i, P_COL_GA // D_MODEL)),
                  pl.BlockSpec((MERGE_TM, D_MODEL), lambda i: (i, P_COL_GB // D_MODEL)),
                  pl.BlockSpec((MERGE_TM, D_MODEL), row),
                  wspec, wspec, wspec, vspec, vspec],
        out_specs=pl.BlockSpec((MERGE_TM, D_MODEL), row),
        out_shape=jax.ShapeDtypeStruct((t, D_MODEL), F32),
        compiler_params=pltpu.CompilerParams(dimension_semantics=("arbitrary",),
                                             vmem_limit_bytes=VMEM_LIMIT),
        name="merge_ln",
    )(ha, hb, p, p, x, wa, wb, wo, g, b)


def _ffn_kernel(x_ref, wu_ref, wd_ref, g_ref, b_ref, out_ref):
    x = x_ref[...]
    xb = x.astype(BF16)
    acc = None
    for j in range(D_FF // FFN_SUB):
        gt = _dot(xb, wu_ref[:, j * FFN_SUB:(j + 1) * FFN_SUB])
        up = _dot(xb, wu_ref[:, D_FF + j * FFN_SUB:D_FF + (j + 1) * FFN_SUB])
        hid = (gt * jax.nn.sigmoid(gt) * up).astype(BF16)
        part = _dot(hid, wd_ref[j * FFN_SUB:(j + 1) * FFN_SUB, :])
        acc = part if acc is None else acc + part
    out_ref[...] = _layernorm(DEEPNORM_ALPHA * x + acc, g_ref[...], b_ref[...])


def _ffn(x, wu, wd, g, b):
    t = x.shape[0]
    row = lambda i: (i, 0)
    const = lambda i: (0, 0)
    return pl.pallas_call(
        _ffn_kernel,
        grid=(t // FFN_TM,),
        in_specs=[pl.BlockSpec((FFN_TM, D_MODEL), row),
                  pl.BlockSpec((D_MODEL, 2 * D_FF), const, pipeline_mode=pl.Buffered(1)),
                  pl.BlockSpec((D_FF, D_MODEL), const, pipeline_mode=pl.Buffered(1)),
                  pl.BlockSpec((1, D_MODEL), const),
                  pl.BlockSpec((1, D_MODEL), const)],
        out_specs=pl.BlockSpec((FFN_TM, D_MODEL), row),
        out_shape=jax.ShapeDtypeStruct((t, D_MODEL), F32),
        compiler_params=pltpu.CompilerParams(dimension_semantics=("arbitrary",),
                                             vmem_limit_bytes=VMEM_LIMIT),
        name="ffn_ln",
    )(x, wu, wd, g, b)


def _merge_ffn_kernel(ha_ref, hb_ref, ga_ref, gb_ref, x_ref, wa_ref, wb_ref, wo_ref, g1_ref, b1_ref,
                      wu_ref, wd_ref, g2_ref, b2_ref, out_ref):
    ya = _dot(ha_ref[...], wa_ref[...])
    yb = _dot(hb_ref[...], wb_ref[...])
    y = jax.nn.sigmoid(ga_ref[...]) * ya + jax.nn.sigmoid(gb_ref[...]) * yb
    mix = _dot(y.astype(BF16), wo_ref[...])
    x1 = _layernorm(DEEPNORM_ALPHA * x_ref[...] + mix, g1_ref[...], b1_ref[...])
    xb = x1.astype(BF16)
    acc = None
    for j in range(D_FF // FFN_SUB):
        gt = _dot(xb, wu_ref[:, j * FFN_SUB:(j + 1) * FFN_SUB])
        up = _dot(xb, wu_ref[:, D_FF + j * FFN_SUB:D_FF + (j + 1) * FFN_SUB])
        hid = (gt * jax.nn.sigmoid(gt) * up).astype(BF16)
        part = _dot(hid, wd_ref[j * FFN_SUB:(j + 1) * FFN_SUB, :])
        acc = part if acc is None else acc + part
    out_ref[...] = _layernorm(DEEPNORM_ALPHA * x1 + acc, g2_ref[...], b2_ref[...])


def _merge_ffn(ha, hb, p, x, wa, wb, wo, g1, b1, wu, wd, g2, b2):
    t = x.shape[0]
    row = lambda i: (i, 0)
    const = lambda i: (0, 0)
    single = dict(pipeline_mode=pl.Buffered(1))
    wspec = pl.BlockSpec((D_MODEL, D_MODEL), const, **single)
    vspec = pl.BlockSpec((1, D_MODEL), const)
    return pl.pallas_call(
        _merge_ffn_kernel,
        grid=(t // FFN_TM,),
        in_specs=[pl.BlockSpec((FFN_TM, A_V), row),
                  pl.BlockSpec((FFN_TM, B_WIDTH), row),
                  pl.BlockSpec((FFN_TM, D_MODEL), lambda i: (i, P_COL_GA // D_MODEL)),
                  pl.BlockSpec((FFN_TM, D_MODEL), lambda i: (i, P_COL_GB // D_MODEL)),
                  pl.BlockSpec((FFN_TM, D_MODEL), row),
                  wspec, wspec, wspec, vspec, vspec,
                  pl.BlockSpec((D_MODEL, 2 * D_FF), const, **single),
                  pl.BlockSpec((D_FF, D_MODEL), const, **single),
                  vspec, vspec],
        out_specs=pl.BlockSpec((FFN_TM, D_MODEL), row),
        out_shape=jax.ShapeDtypeStruct((t, D_MODEL), F32),
        compiler_params=pltpu.CompilerParams(dimension_semantics=("arbitrary",),
                                             vmem_limit_bytes=VMEM_LIMIT),
        name="merge_ffn_ln",
    )(ha, hb, p, p, x, wa, wb, wo, g1, b1, wu, wd, g2, b2)


def _gate_vec(parts, width):
    out = jnp.zeros((DEPTH, width), F32)
    for off, val in parts:
        out = out.at[:, off:off + val.shape[1]].set(val.astype(F32))
    return out


def kernel(x, w_in, b_igate, b_fgate, g_mlstm_norm, conv_w, a_log, dt_bias, g_gdn_norm, w_branch_a, w_branch_b, w_out, ln1_g, ln1_b, w_ffn_up, w_ffn_down, ln2_g, ln2_b):
    batch, seq, _ = x.shape
    t = batch * seq

    o_va = 2 * A_QK
    o_oa = o_va + A_V
    o_if = o_oa + A_V
    o_qb = o_if + 2 * A_HEADS
    o_kb, o_vb, o_zb = o_qb + B_WIDTH, o_qb + 2 * B_WIDTH, o_qb + 3 * B_WIDTH
    o_beta = o_zb + B_WIDTH
    o_gate = o_beta + 2 * B_HEADS
    w_big = jnp.concatenate([w_in[:, :, o_qb:o_kb], w_in[:, :, 0:o_oa],
                             w_in[:, :, o_kb:o_vb], w_in[:, :, o_oa:o_if], w_in[:, :, o_zb:o_beta],
                             w_in[:, :, o_vb:o_zb], w_in[:, :, o_gate:]], axis=-1).astype(BF16)
    w_gates = jnp.concatenate([w_in[:, :, o_if:o_qb], w_in[:, :, o_beta:o_gate]], axis=-1)
    conv_blocks = jnp.swapaxes(conv_w.astype(F32).reshape(DEPTH, CONV_K, 3, CONV_COLS), 1, 2)
    n_g = w_gates.shape[-1]
    w_small = jnp.pad(w_gates, ((0, 0), (0, 0), (0, GATE_LANES - n_g))).astype(BF16)
    w_small_t = jnp.pad(jnp.swapaxes(w_gates, 1, 2), ((0, 0), (0, GATE_ROWS - n_g), (0, 0))).astype(BF16)

    bias_row = _gate_vec([(LANE_I, b_igate), (LANE_F, b_fgate), (LANE_A, dt_bias)], GATE_LANES)
    bias_col = _gate_vec([(LANE_I, b_igate), (LANE_F, b_fgate), (LANE_A, dt_bias)], GATE_ROWS)
    alog_row = _gate_vec([(LANE_A, a_log)], GATE_LANES)
    alog_col = _gate_vec([(LANE_A, a_log)], GATE_ROWS)

    wa = w_branch_a.astype(BF16)
    wb = w_branch_b.astype(BF16)
    wo = w_out.astype(BF16)
    wu = w_ffn_up.astype(BF16)
    wd = w_ffn_down.astype(BF16)

    xt = x.reshape(t, D_MODEL)
    for l in range(DEPTH):
        p, pc = _proj(xt, w_big[l], conv_blocks[l], seq)
        ps, pst = _gate_proj(xt, w_small[l], w_small_t[l])
        brow = bias_row[l][None, :]
        bcol = bias_col[l][:, None]
        ha = _mlstm(p, ps, pst, brow, bcol, g_mlstm_norm[l][None, :].astype(F32), batch, seq)
        hb = _gdn(p, pc, ps, pst, brow, bcol, alog_row[l][None, :], alog_col[l][:, None],
                  g_gdn_norm[l][None, :].astype(F32), batch, seq)
        xt = _merge_ffn(ha, hb, p, xt, wa[l], wb[l], wo[l], ln1_g[l][None, :], ln1_b[l][None, :],
                        wu[l], wd[l], ln2_g[l][None, :], ln2_b[l][None, :])
    return xt.reshape(batch, seq, D_MODEL)
```
